```python
import math
import jax, jax.numpy as jnp
from jax import lax
import numpy as np

D_MODEL = 2048
BATCH = 4
SEQ = 4096
DEPTH = 2

CHUNK = 64
Q_BLOCK = 128
EPS = 1e-6
NEG_INF = -1e30

SB_HEADS = 8
SB_HEAD_DIM = 64
SB_WIDTH = SB_HEADS * SB_HEAD_DIM

SSM_WIDTH = 512
SSM_GROUP = 16
SSM_GROUPS = SSM_WIDTH // SSM_GROUP
SSM_STATE = 64
DT_MIN = 1e-3
DT_MAX = 1e-1

MLA_HEADS = 8
MLA_NOPE = 128
MLA_ROPE = 64
MLA_V = 128
MLA_Q_RANK = 512
MLA_KV_RANK = 256
MLA_WIDTH = MLA_HEADS * MLA_V
ROPE_BASE = 10000.0

D_MIX = SB_WIDTH + SSM_WIDTH + MLA_WIDTH
IN_SPLITS = (SB_WIDTH, 2 * SB_WIDTH, 3 * SB_WIDTH, 3 * SB_WIDTH + SSM_WIDTH,
             3 * SB_WIDTH + SSM_WIDTH + MLA_Q_RANK,
             3 * SB_WIDTH + SSM_WIDTH + MLA_Q_RANK + MLA_KV_RANK)
D_IN = IN_SPLITS[-1] + MLA_ROPE

D_FF = 5504

kernel_name = 'hybrid_sb_s5_mla_macaron_encoder'


def rmsnorm(x, w):
    xf = x.astype(jnp.float32)
    xf = xf * lax.rsqrt(jnp.mean(xf * xf, axis=-1, keepdims=True) + EPS)
    return xf.astype(x.dtype) * w


def swiglu(h, w13, w2):
    a, b = jnp.split(h @ w13, 2, axis=-1)
    return (jax.nn.silu(a) * b) @ w2


def rope_tables(seq_len):
    inv = ROPE_BASE ** (-jnp.arange(0, MLA_ROPE, 2, dtype=jnp.float32) / MLA_ROPE)
    ang = jnp.arange(seq_len, dtype=jnp.float32)[:, None] * inv[None, :]
    return jnp.cos(ang), jnp.sin(ang)


def apply_rope(x, cos, sin):
    x1, x2 = jnp.split(x.astype(jnp.float32), 2, axis=-1)
    return jnp.concatenate([x1 * cos - x2 * sin, x1 * sin + x2 * cos], axis=-1).astype(x.dtype)


def _to_blocks(q):
    b, h, s, d = q.shape
    return q.reshape(b, h, s // Q_BLOCK, Q_BLOCK, d).transpose(2, 0, 1, 3, 4)


def _from_blocks(o):
    nb, b, h, qb, d = o.shape
    return o.transpose(1, 2, 0, 3, 4).reshape(b, h, nb * qb, d)


def stick_breaking_attention(q, k, v):
    s_len, dh = q.shape[2], q.shape[3]
    scale = dh ** -0.5
    kpos = jnp.arange(s_len)

    def block(args):
        qb, i = args
        qpos = i * Q_BLOCK + jnp.arange(Q_BLOCK)
        z = jnp.einsum('bhqd,bhkd->bhqk', qb, k).astype(jnp.float32) * scale
        strict = kpos[None, :] < qpos[:, None]
        log_1m_beta = jnp.where(strict, jax.nn.log_sigmoid(-z), 0.0)
        tail = lax.cumsum(log_1m_beta, axis=3, reverse=True) - log_1m_beta
        w = jnp.where(strict, jnp.exp(jax.nn.log_sigmoid(z) + tail), 0.0)
        return jnp.einsum('bhqk,bhkd->bhqd', w.astype(v.dtype), v)

    nb = s_len // Q_BLOCK
    return _from_blocks(lax.map(block, (_to_blocks(q), jnp.arange(nb))))


def chunk_causal_attention(q, k, v):
    s_len, dk = q.shape[2], q.shape[3]
    scale = dk ** -0.5
    kchunk = jnp.arange(s_len) // CHUNK

    def block(args):
        qb, i = args
        qchunk = (i * Q_BLOCK + jnp.arange(Q_BLOCK)) // CHUNK
        sc = jnp.einsum('bhqd,bhkd->bhqk', qb, k).astype(jnp.float32) * scale
        sc = jnp.where(kchunk[None, :] <= qchunk[:, None], sc, NEG_INF)
        p = jax.nn.softmax(sc, axis=-1)
        return jnp.einsum('bhqk,bhkd->bhqd', p.astype(v.dtype), v)

    nb = s_len // Q_BLOCK
    return _from_blocks(lax.map(block, (_to_blocks(q), jnp.arange(nb))))


def s5_glu(u, lam_re, lam_im, log_dt, b_re, b_im, c_re, c_im, d_skip, w_glu):
    b, s, _ = u.shape
    f32 = jnp.float32
    uf = u.astype(f32).reshape(b, s, SSM_GROUPS, SSM_GROUP)
    lam = lax.complex(lam_re.astype(f32), lam_im.astype(f32))
    dt = jnp.exp(log_dt.astype(f32))[:, None]
    lam_bar = jnp.exp(lam * dt)
    b_mat = lax.complex(b_re.astype(f32), b_im.astype(f32))
    b_bar = ((lam_bar - 1.0) / lam)[..., None] * b_mat
    c_mat = lax.complex(c_re.astype(f32), c_im.astype(f32))
    bu = jnp.einsum('gph,bsgh->bsgp', b_bar, uf.astype(jnp.complex64))
    a = jnp.broadcast_to(lam_bar, bu.shape)

    def combine(left, right):
        a1, x1 = left
        a2, x2 = right
        return a1 * a2, a2 * x1 + x2

    _, states = lax.associative_scan(combine, (a, bu), axis=1)
    y = jnp.einsum('ghp,bsgp->bsgh', c_mat, states).real
    y = y + d_skip.astype(f32).reshape(SSM_GROUPS, SSM_GROUP) * uf
    y = y.reshape(b, s, SSM_WIDTH).astype(u.dtype)
    g = jax.nn.gelu(y)
    return g * jax.nn.sigmoid(g @ w_glu)


def hybrid_mixer(h, cos, sin, w_in, mla_q_norm_w, mla_w_uq, mla_kv_norm_w, mla_w_ukv,
                 ssm_lam_re, ssm_lam_im, ssm_log_dt, ssm_b_re, ssm_b_im, ssm_c_re, ssm_c_im,
                 ssm_d, ssm_w_glu, mix_norm_w, w_out):
    b, s, _ = h.shape
    q_sb, k_sb, v_sb, u, c_q, c_kv, k_r = jnp.split(h @ w_in, IN_SPLITS, axis=-1)

    def heads(t, n):
        return t.reshape(b, s, n, -1).transpose(0, 2, 1, 3)

    def merge(t):
        return t.transpose(0, 2, 1, 3).reshape(b, s, -1)

    y_sb = merge(stick_breaking_attention(heads(q_sb, SB_HEADS), heads(k_sb, SB_HEADS), heads(v_sb, SB_HEADS)))

    y_ssm = s5_glu(u, ssm_lam_re, ssm_lam_im, ssm_log_dt, ssm_b_re, ssm_b_im,
                   ssm_c_re, ssm_c_im, ssm_d, ssm_w_glu)

    q = heads(rmsnorm(c_q, mla_q_norm_w) @ mla_w_uq, MLA_HEADS)
    q_nope, q_rope = jnp.split(q, [MLA_NOPE], axis=-1)
    kv = heads(rmsnorm(c_kv, mla_kv_norm_w) @ mla_w_ukv, MLA_HEADS)
    k_nope, v = jnp.split(kv, [MLA_NOPE], axis=-1)
    k_rope = jnp.broadcast_to(apply_rope(k_r[:, None], cos, sin), (b, MLA_HEADS, s, MLA_ROPE))
    q_full = jnp.concatenate([q_nope, apply_rope(q_rope, cos, sin)], axis=-1)
    k_full = jnp.concatenate([k_nope, k_rope], axis=-1)
    y_mla = merge(chunk_causal_attention(q_full, k_full, v))

    g_sb, g_ssm, g_mla = jnp.split(mix_norm_w, [SB_WIDTH, SB_WIDTH + SSM_WIDTH])
    y = jnp.concatenate([rmsnorm(y_sb, g_sb), rmsnorm(y_ssm, g_ssm), rmsnorm(y_mla, g_mla)], axis=-1)
    return y @ w_out


def _normal(key, shape, scale):
    return jax.random.normal(key, shape, jnp.float32) * scale


def setup_inputs(seed: int = 0) -> dict:
    key = jax.random.key(seed)
    ks = jax.random.split(key, 24)
    L = DEPTH
    G, P, Hg = SSM_GROUPS, SSM_STATE, SSM_GROUP
    n = jnp.arange(P, dtype=jnp.float32)
    return {
        'x': jax.random.normal(ks[0], (BATCH, SEQ, D_MODEL), jnp.float32),
        'norm_w': 1.0 + _normal(ks[1], (L, 3, D_MODEL), 0.02),
        'ffn_w13': _normal(ks[2], (L, 2, D_MODEL, 2 * D_FF), D_MODEL ** -0.5),
        'ffn_w2': _normal(ks[3], (L, 2, D_FF, D_MODEL), D_FF ** -0.5),
        'w_in': _normal(ks[4], (L, D_MODEL, D_IN), D_MODEL ** -0.5),
        'mla_q_norm_w': 1.0 + _normal(ks[5], (L, MLA_Q_RANK), 0.02),
        'mla_w_uq': _normal(ks[6], (L, MLA_Q_RANK, MLA_HEADS * (MLA_NOPE + MLA_ROPE)), MLA_Q_RANK ** -0.5),
        'mla_kv_norm_w': 1.0 + _normal(ks[7], (L, MLA_KV_RANK), 0.02),
        'mla_w_ukv': _normal(ks[8], (L, MLA_KV_RANK, MLA_HEADS * (MLA_NOPE + MLA_V)), MLA_KV_RANK ** -0.5),
        'ssm_lam_re': -0.5 + _normal(ks[9], (L, G, P), 0.01),
        'ssm_lam_im': math.pi * n + _normal(ks[10], (L, G, P), 0.01),
        'ssm_log_dt': jax.random.uniform(ks[11], (L, G), jnp.float32, math.log(DT_MIN), math.log(DT_MAX)),
        'ssm_b_re': _normal(ks[12], (L, G, P, Hg), (2 * Hg) ** -0.5),
        'ssm_b_im': _normal(ks[13], (L, G, P, Hg), (2 * Hg) ** -0.5),
        'ssm_c_re': _normal(ks[14], (L, G, Hg, P), (2 * P) ** -0.5),
        'ssm_c_im': _normal(ks[15], (L, G, Hg, P), (2 * P) ** -0.5),
        'ssm_d': _normal(ks[16], (L, SSM_WIDTH), 1.0),
        'ssm_w_glu': _normal(ks[17], (L, SSM_WIDTH, SSM_WIDTH), SSM_WIDTH ** -0.5),
        'mix_norm_w': 1.0 + _normal(ks[18], (L, D_MIX), 0.02),
        'w_out': _normal(ks[19], (L, D_MIX, D_MODEL), D_MIX ** -0.5),
        'final_norm_w': 1.0 + _normal(ks[20], (D_MODEL,), 0.02),
    }


def reference(x, norm_w, ffn_w13, ffn_w2, w_in, mla_q_norm_w, mla_w_uq, mla_kv_norm_w, mla_w_ukv,
              ssm_lam_re, ssm_lam_im, ssm_log_dt, ssm_b_re, ssm_b_im, ssm_c_re, ssm_c_im,
              ssm_d, ssm_w_glu, mix_norm_w, w_out, final_norm_w):
    cos, sin = rope_tables(x.shape[1])
    for l in range(DEPTH):
        x = x + 0.5 * swiglu(rmsnorm(x, norm_w[l, 0]), ffn_w13[l, 0], ffn_w2[l, 0])
        x = x + hybrid_mixer(rmsnorm(x, norm_w[l, 1]), cos, sin, w_in[l],
                             mla_q_norm_w[l], mla_w_uq[l], mla_kv_norm_w[l], mla_w_ukv[l],
                             ssm_lam_re[l], ssm_lam_im[l], ssm_log_dt[l], ssm_b_re[l], ssm_b_im[l],
                             ssm_c_re[l], ssm_c_im[l], ssm_d[l], ssm_w_glu[l], mix_norm_w[l], w_out[l])
        x = x + 0.5 * swiglu(rmsnorm(x, norm_w[l, 2]), ffn_w13[l, 1], ffn_w2[l, 1])
    return rmsnorm(x, final_norm_w)
```

```python
import functools
import math

import jax
import jax.numpy as jnp
from jax import lax
from jax.experimental import pallas as pl
from jax.experimental.pallas import tpu as pltpu

F32 = jnp.float32
BF16 = jnp.bfloat16

EPS = 1e-6
NEG_INF = -1e30
LANE = 128
MXU_DIM = 256
VMEM_LIMIT = 56 * 1024 * 1024

CHUNK = 64
SB_HEADS = 8
SB_HEAD_DIM = 64
SB_WIDTH = SB_HEADS * SB_HEAD_DIM
SSM_WIDTH = 512
SSM_GROUP = 16
SSM_GROUPS = SSM_WIDTH // SSM_GROUP
SSM_STATE = 64
MLA_HEADS = 8
MLA_NOPE = 128
MLA_ROPE = 64
MLA_V = 128
MLA_Q_RANK = 512
MLA_KV_RANK = 256
MLA_WIDTH = MLA_HEADS * MLA_V
ROPE_BASE = 10000.0
ROPE_HALF = MLA_ROPE // 2
MLA_QW = MLA_NOPE + LANE

S5_L = 16
S5_W = S5_L * SSM_GROUP


def _cparams(sem):
    return pltpu.CompilerParams(dimension_semantics=sem, vmem_limit_bytes=VMEM_LIMIT)


def _rms(x, w):
    ms = jnp.mean(x * x, axis=-1, keepdims=True)
    return x * lax.rsqrt(ms + EPS) * w


def _const_spec(shape):
    nd = len(shape)
    return pl.BlockSpec(shape, lambda *_: (0,) * nd, pipeline_mode=pl.Buffered(1))


def _ffn_kernel(x_ref, nw_ref, w13_ref, w2_ref, fw_ref, o_ref, h_ref, *, tf, final):
    j = pl.program_id(1)

    @pl.when(j == 0)
    def _():
        x = x_ref[...]
        h_ref[...] = _rms(x, nw_ref[...]).astype(BF16)
        o_ref[...] = x

    ab = jnp.dot(h_ref[...], w13_ref[...], preferred_element_type=F32)
    a = ab[:, :tf]
    b = ab[:, tf:]
    g = (a * jax.nn.sigmoid(a)) * b * 0.5
    o_ref[...] += jnp.dot(g.astype(BF16), w2_ref[...], preferred_element_type=F32)

    if final:
        @pl.when(j == pl.num_programs(1) - 1)
        def _():
            o_ref[...] = _rms(o_ref[...], fw_ref[...])


def _ffn(x, nw, w13p, w2p, fw, *, tm, tf, final):
    t, d = x.shape
    nt = w2p.shape[0] // tf
    return pl.pallas_call(
        functools.partial(_ffn_kernel, tf=tf, final=final),
        grid=(t // tm, nt),
        in_specs=[
            pl.BlockSpec((tm, d), lambda i, j: (i, 0)),
            pl.BlockSpec((1, d), lambda i, j: (0, 0)),
            pl.BlockSpec((d, 2 * tf), lambda i, j: (0, j)),
            pl.BlockSpec((tf, d), lambda i, j: (j, 0)),
            pl.BlockSpec((1, d), lambda i, j: (0, 0)),
        ],
        out_specs=pl.BlockSpec((tm, d), lambda i, j: (i, 0)),
        out_shape=jax.ShapeDtypeStruct((t, d), F32),
        scratch_shapes=[pltpu.VMEM((tm, d), BF16)],
        compiler_params=_cparams(("parallel", "arbitrary")),
        name="ffn",
    )(x, nw, w13p, w2p, fw)


def _prep_ffn(w13, w2, tf):
    d, two_ff = w13.shape
    ff = two_ff // 2
    nt = -(-ff // tf)
    pad = nt * tf - ff
    w1 = jnp.pad(w13[:, :ff], ((0, 0), (0, pad))).reshape(d, nt, tf)
    w3 = jnp.pad(w13[:, ff:], ((0, 0), (0, pad))).reshape(d, nt, tf)
    w13p = jnp.concatenate([w1, w3], axis=2).reshape(d, nt * 2 * tf).astype(BF16)
    w2p = jnp.pad(w2, ((0, pad), (0, 0))).astype(BF16)
    return w13p, w2p


def _rope(r, tc, ts):
    return r * tc + pltpu.roll(r, 2 * ROPE_HALF, 1) * ts


def _inproj_kernel(x_ref, nw_ref, win_ref, qnw_ref, wuq_ref, kvnw_ref, wukv_ref, tc_ref, ts_ref,
                   qkv_ref, u_ref, qc_ref, kv_ref, kr_ref):
    h = _rms(x_ref[...], nw_ref[...]).astype(BF16)
    y = jnp.dot(h, win_ref[...], preferred_element_type=F32)
    o_u = 3 * SB_WIDTH
    o_cq = o_u + SSM_WIDTH
    o_ckv = o_cq + MLA_Q_RANK
    o_kr = o_ckv + MLA_KV_RANK
    qkv_ref[:, :SB_WIDTH] = (y[:, :SB_WIDTH] * (SB_HEAD_DIM ** -0.5)).astype(BF16)
    qkv_ref[:, SB_WIDTH:] = y[:, SB_WIDTH:o_u].astype(BF16)
    u_ref[...] = y[:, o_u:o_cq]
    tc = tc_ref[...]
    ts = ts_ref[...]
    cq = _rms(y[:, o_cq:o_ckv], qnw_ref[...]).astype(BF16)
    q = jnp.dot(cq, wuq_ref[...], preferred_element_type=F32) * ((MLA_NOPE + MLA_ROPE) ** -0.5)
    for hh in range(MLA_HEADS):
        o = hh * MLA_QW
        qc_ref[:, o:o + MLA_NOPE] = q[:, o:o + MLA_NOPE].astype(BF16)
        qc_ref[:, o + MLA_NOPE:o + MLA_QW] = _rope(q[:, o + MLA_NOPE:o + MLA_QW], tc, ts).astype(BF16)
    ckv = _rms(y[:, o_ckv:o_kr], kvnw_ref[...]).astype(BF16)
    kv_ref[...] = jnp.dot(ckv, wukv_ref[...], preferred_element_type=F32).astype(BF16)
    kr_ref[...] = _rope(y[:, o_kr:o_kr + LANE], tc, ts).astype(BF16)


def _inproj(x, nw, winp, qnw, wuqp, kvnw, wukv, tabc, tabs, *, tm, seq):
    t, d = x.shape
    nseq = seq // tm
    row = lambda i: (i, 0)
    return pl.pallas_call(
        _inproj_kernel,
        grid=(t // tm,),
        in_specs=[
            pl.BlockSpec((tm, d), row),
            _const_spec((1, d)),
            _const_spec(winp.shape),
            _const_spec((1, MLA_Q_RANK)),
            _const_spec(wuqp.shape),
            _const_spec((1, MLA_KV_RANK)),
            _const_spec(wukv.shape),
            pl.BlockSpec((tm, LANE), lambda i: (i % nseq, 0)),
            pl.BlockSpec((tm, LANE), lambda i: (i % nseq, 0)),
        ],
        out_specs=[
            pl.BlockSpec((tm, 3 * SB_WIDTH), row),
            pl.BlockSpec((tm, SSM_WIDTH), row),
            pl.BlockSpec((tm, MLA_HEADS * MLA_QW), row),
            pl.BlockSpec((tm, MLA_HEADS * (MLA_NOPE + MLA_V)), row),
            pl.BlockSpec((tm, LANE), row),
        ],
        out_shape=[
            jax.ShapeDtypeStruct((t, 3 * SB_WIDTH), BF16),
            jax.ShapeDtypeStruct((t, SSM_WIDTH), F32),
            jax.ShapeDtypeStruct((t, MLA_HEADS * MLA_QW), BF16),
            jax.ShapeDtypeStruct((t, MLA_HEADS * (MLA_NOPE + MLA_V)), BF16),
            jax.ShapeDtypeStruct((t, LANE), BF16),
        ],
        compiler_params=_cparams(("parallel",)),
        name="inproj",
    )(x, nw, winp, qnw, wuqp, kvnw, wukv, tabc, tabs)


def _rope_pad(w):
    z = jnp.zeros(w.shape[:-1] + (ROPE_HALF,), w.dtype)
    return jnp.concatenate([w[..., :ROPE_HALF], z, w[..., ROPE_HALF:], z], axis=-1)


def _prep_inproj(w_in, w_uq, w_ukv):
    o_kr = 3 * SB_WIDTH + SSM_WIDTH + MLA_Q_RANK + MLA_KV_RANK
    winp = jnp.concatenate([w_in[:, :o_kr], _rope_pad(w_in[:, o_kr:])], axis=1).astype(BF16)
    wq = w_uq.reshape(MLA_Q_RANK, MLA_HEADS, MLA_NOPE + MLA_ROPE)
    wuqp = jnp.concatenate([wq[..., :MLA_NOPE], _rope_pad(wq[..., MLA_NOPE:])], axis=-1)
    wuqp = wuqp.reshape(MLA_Q_RANK, MLA_HEADS * MLA_QW).astype(BF16)
    return winp, wuqp, w_ukv.astype(BF16)


def _rope_tables(seq):
    inv = ROPE_BASE ** (-jnp.arange(0, MLA_ROPE, 2, dtype=F32) / MLA_ROPE)
    ang = jnp.arange(seq, dtype=F32)[:, None] * inv[None, :]
    c, s = jnp.cos(ang), jnp.sin(ang)
    z = jnp.zeros_like(c)
    return jnp.concatenate([c, z, c, z], axis=1), jnp.concatenate([-s, z, s, z], axis=1)


def _sb_kernel(q_ref, k_ref, v_ref, u_ref, o_ref, *, tq, tk):
    i = pl.program_id(2)
    nd = tq // tk
    lane = lax.broadcasted_iota(jnp.int32, (1, LANE), 1)
    q = q_ref[...]
    umat = u_ref[...]
    rowpos = i * tq + lax.broadcasted_iota(jnp.int32, (tq, 1), 0)
    colofs = lax.broadcasted_iota(jnp.int32, (1, tk), 1)
    total = jnp.zeros((tq, LANE), F32)

    for hh in range(2):
        hm = (lane // SB_HEAD_DIM) == hh
        qh = jnp.where(hm, q, jnp.zeros_like(q))

        def block(kb, carry, acc, masked, qh=qh, hm=hm):
            ks = pl.multiple_of(kb * tk, tk)
            kblk = k_ref[pl.ds(ks, tk), :]
            vblk = v_ref[pl.ds(ks, tk), :]
            vblk = jnp.where(hm, vblk, jnp.zeros_like(vblk))
            z = lax.dot_general(qh, kblk, (((1,), (1,)), ((), ())), preferred_element_type=F32)
            sp = jnp.maximum(z, 0.0) + jnp.log(1.0 + jnp.exp(-jnp.abs(z)))
            if masked:
                strict = (ks + colofs) < rowpos
                sp = jnp.where(strict, sp, 0.0)
            hi = sp.astype(BF16)
            lo = (sp - hi.astype(F32)).astype(BF16)
            tail = (jnp.dot(hi, umat, preferred_element_type=F32)
                    + jnp.dot(lo, umat, preferred_element_type=F32))
            w = jnp.exp(z - sp - tail - carry)
            if masked:
                w = jnp.where(strict, w, 0.0)
            acc = acc + jnp.dot(w.astype(BF16), vblk, preferred_element_type=F32)
            carry = carry + jnp.sum(sp, axis=-1, keepdims=True)
            return carry, acc

        carry = jnp.zeros((tq, 1), F32)
        acc = jnp.zeros((tq, LANE), F32)
        for dblk in range(nd):
            carry, acc = block(i * nd + (nd - 1 - dblk), carry, acc, True)

        def body(t, c):
            return block(i * nd - 1 - t, c[0], c[1], False)

        carry, acc = lax.fori_loop(0, i * nd, body, (carry, acc))
        total = total + acc

    o_ref[...] = total


def _sb_attention(qkv, umat, *, batch, seq, tq, tk):
    t = qkv.shape[0]
    nq = seq // tq
    npair = SB_WIDTH // LANE
    return pl.pallas_call(
        functools.partial(_sb_kernel, tq=tq, tk=tk),
        grid=(batch, npair, nq),
        in_specs=[
            pl.BlockSpec((tq, LANE), lambda b, p, i: (b * nq + i, p)),
            pl.BlockSpec((seq, LANE), lambda b, p, i: (b, npair + p)),
            pl.BlockSpec((seq, LANE), lambda b, p, i: (b, 2 * npair + p)),
            pl.BlockSpec((tk, tk), lambda b, p, i: (0, 0)),
        ],
        out_specs=pl.BlockSpec((tq, LANE), lambda b, p, i: (b * nq + i, p)),
        out_shape=jax.ShapeDtypeStruct((t, SB_WIDTH), F32),
        compiler_params=_cparams(("parallel", "parallel", "arbitrary")),
        name="sb_attn",
    )(qkv, qkv, qkv, umat)


def _mla_kernel(q_ref, kn_ref, kr_ref, v_ref, o_ref, *, tq):
    i = pl.program_id(2)
    tk = tq
    q = q_ref[...]
    qchunk = (i * tq + lax.broadcasted_iota(jnp.int32, (tq, 1), 0)) // CHUNK
    colofs = lax.broadcasted_iota(jnp.int32, (1, tk), 1)

    def block(kb, m, l, acc, masked):
        ks = pl.multiple_of(kb * tk, tk)
        kcat = jnp.concatenate([kn_ref[pl.ds(ks, tk), :], kr_ref[pl.ds(ks, tk), :]], axis=1)
        s = lax.dot_general(q, kcat, (((1,), (1,)), ((), ())), preferred_element_type=F32)
        if masked:
            s = jnp.where((ks + colofs) // CHUNK <= qchunk, s, NEG_INF)
        m_new = jnp.maximum(m, jnp.max(s, axis=-1, keepdims=True))
        alpha = jnp.exp(m - m_new)
        p = jnp.exp(s - m_new)
        l = alpha * l + jnp.sum(p, axis=-1, keepdims=True)
        acc = alpha * acc + jnp.dot(p.astype(BF16), v_ref[pl.ds(ks, tk), :], preferred_element_type=F32)
        return m_new, l, acc

    m = jnp.full((tq, 1), NEG_INF, F32)
    l = jnp.zeros((tq, 1), F32)
    acc = jnp.zeros((tq, MLA_V), F32)
    m, l, acc = block(i, m, l, acc, True)

    def body(t, c):
        return block(i - 1 - t, c[0], c[1], c[2], False)

    m, l, acc = lax.fori_loop(0, i, body, (m, l, acc))
    o_ref[...] = acc / l


def _mla_attention(qc, kv, kr, *, batch, seq, tq):
    t = qc.shape[0]
    nq = seq // tq
    return pl.pallas_call(
        functools.partial(_mla_kernel, tq=tq),
        grid=(batch, MLA_HEADS, nq),
        in_specs=[
            pl.BlockSpec((tq, MLA_QW), lambda b, h, i: (b * nq + i, h)),
            pl.BlockSpec((seq, MLA_NOPE), lambda b, h, i: (b, 2 * h)),
            pl.BlockSpec((seq, LANE), lambda b, h, i: (b, 0)),
            pl.BlockSpec((seq, MLA_V), lambda b, h, i: (b, 2 * h + 1)),
        ],
        out_specs=pl.BlockSpec((tq, MLA_V), lambda b, h, i: (b * nq + i, h)),
        out_shape=jax.ShapeDtypeStruct((t, MLA_WIDTH), F32),
        compiler_params=_cparams(("parallel", "parallel", "arbitrary")),
        name="mla_attn",
    )(qc, kv, kr, kv)


def _cmul(ar, ai, br, bi):
    return ar * br - ai * bi, ar * bi + ai * br


def _s5_prep_kernel(lrc_ref, lic_ref, lrr_ref, lir_ref, ldt_ref, btr_ref, bti_ref, ctr_ref, cti_ref,
                    tmat_ref, bmat_ref, cmat_ref, pa_ref, pb_ref):
    dt = jnp.exp(ldt_ref[0])
    ar = lrc_ref[0] * dt
    ai = lic_ref[0] * dt
    step_i = (lax.broadcasted_iota(jnp.int32, (1, S5_W), 1) // SSM_GROUP).astype(F32)
    mag = jnp.exp(ar * step_i)
    er, ei = mag * jnp.cos(ai * step_i), mag * jnp.sin(ai * step_i)
    cer, cei = _cmul(ctr_ref[0], cti_ref[0], er, ei)
    cmat0 = jnp.concatenate([cer, -cei], axis=0)
    l1r, l1i = jnp.exp(ar) * jnp.cos(ai), jnp.exp(ar) * jnp.sin(ai)
    c1r, c1i = _cmul(cer, cei, l1r, l1i)
    cmat_ref[0] = jnp.concatenate([c1r, -c1i], axis=0).astype(BF16)
    lam_r = lrr_ref[0]
    lam_i = lir_ref[0]
    arr = lam_r * dt
    air = lam_i * dt
    lbr, lbi = jnp.exp(arr) * jnp.cos(air), jnp.exp(arr) * jnp.sin(air)
    nr, ni = lbr - 1.0, lbi
    den = lam_r * lam_r + lam_i * lam_i
    qr = (nr * lam_r + ni * lam_i) / den
    qi = (ni * lam_r - nr * lam_i) / den
    back_j = (S5_L - 1 - lax.broadcasted_iota(jnp.int32, (S5_W, 1), 0) // SSM_GROUP).astype(F32)
    magr = jnp.exp(arr * back_j)
    fr, fi = magr * jnp.cos(air * back_j), magr * jnp.sin(air * back_j)
    wr, wi = _cmul(qr, qi, fr, fi)
    br, bi = _cmul(wr, wi, btr_ref[0], bti_ref[0])
    is_re = lax.broadcasted_iota(jnp.int32, (1, 2 * SSM_STATE), 1) < SSM_STATE
    bmat = jnp.where(is_re, br, bi)
    bmat_ref[0] = bmat.astype(BF16)
    bbar_t = bmat[S5_W - SSM_GROUP:, :]
    r0 = jnp.dot(bbar_t, cmat0, preferred_element_type=F32, precision=lax.Precision.HIGHEST)
    lane = lax.broadcasted_iota(jnp.int32, (1, S5_W), 1)
    for j in range(S5_L):
        blk = r0 if j == 0 else jnp.where(lane >= j * SSM_GROUP, pltpu.roll(r0, j * SSM_GROUP, 1), 0.0)
        tmat_ref[0, j * SSM_GROUP:(j + 1) * SSM_GROUP, :] = blk.astype(BF16)
    zr, zi = lbr, lbi
    for _ in range(int(math.log2(S5_L))):
        zr, zi = _cmul(zr, zi, zr, zi)
    for k in range(pa_ref.shape[1]):
        pa_ref[0, k:k + 1, :] = zr
        pb_ref[0, k:k + 1, :] = jnp.where(is_re, -zi, zi)
        zr, zi = _cmul(zr, zi, zr, zi)


def _s5_prep(lam_re, lam_im, log_dt, b_re, b_im, c_re, c_im, *, steps):
    g, p = lam_re.shape
    dup = lambda a: jnp.concatenate([a, a], axis=-1)
    bt = lambda b: dup(jnp.tile(jnp.swapaxes(b, 1, 2), (1, S5_L, 1)))
    ct = lambda c: jnp.tile(jnp.swapaxes(c, 1, 2), (1, 1, S5_L))
    blk = lambda s: pl.BlockSpec((1,) + s, lambda i: (i, 0, 0))
    return pl.pallas_call(
        _s5_prep_kernel,
        grid=(g,),
        in_specs=[blk((p, 1)), blk((p, 1)), blk((1, 2 * p)), blk((1, 2 * p)), blk((1, 1)),
                  blk((S5_W, 2 * p)), blk((S5_W, 2 * p)), blk((p, S5_W)), blk((p, S5_W))],
        out_specs=[blk((S5_W, S5_W)), blk((S5_W, 2 * p)), blk((2 * p, S5_W)),
                   blk((steps, 2 * p)), blk((steps, 2 * p))],
        out_shape=[
            jax.ShapeDtypeStruct((g, S5_W, S5_W), BF16),
            jax.ShapeDtypeStruct((g, S5_W, 2 * p), BF16),
            jax.ShapeDtypeStruct((g, 2 * p, S5_W), BF16),
            jax.ShapeDtypeStruct((g, steps, 2 * p), F32),
            jax.ShapeDtypeStruct((g, steps, 2 * p), F32),
        ],
        compiler_params=_cparams(("parallel",)),
        name="s5_prep",
    )(lam_re[:, :, None], lam_im[:, :, None], dup(lam_re)[:, None, :], dup(lam_im)[:, None, :],
      log_dt[:, None, None], bt(b_re), bt(b_im), ct(c_re), ct(c_im))


def _s5_scan_kernel(u_ref, tmat_ref, bmat_ref, cmat_ref, pa_ref, pb_ref, y_ref, *, nchunk):
    u = u_ref[0]
    rows = u.shape[0]
    x = jnp.dot(u, bmat_ref[0], preferred_element_type=F32)
    cidx = lax.broadcasted_iota(jnp.int32, (rows, 1), 0) % nchunk
    pa = pa_ref[0]
    pb = pb_ref[0]
    for k in range(pa.shape[0]):
        sh = 1 << k
        xs = jnp.where(cidx >= sh, pltpu.roll(x, sh, 0), 0.0)
        x = x + pa[k:k + 1, :] * xs + pb[k:k + 1, :] * pltpu.roll(xs, SSM_STATE, 1)
    s_in = jnp.where(cidx >= 1, pltpu.roll(x, 1, 0), 0.0)
    y_ref[0] = (jnp.dot(u, tmat_ref[0], preferred_element_type=F32)
                + jnp.dot(s_in.astype(BF16), cmat_ref[0], preferred_element_type=F32))


def _s5_scan(ug, tmat, bmat, cmat, pa, pb, *, nchunk):
    g, rows, w = ug.shape
    blk = lambda s: pl.BlockSpec((1,) + s, lambda i: (i, 0, 0))
    return pl.pallas_call(
        functools.partial(_s5_scan_kernel, nchunk=nchunk),
        grid=(g,),
        in_specs=[blk((rows, w)), blk(tmat.shape[1:]), blk(bmat.shape[1:]), blk(cmat.shape[1:]),
                  blk(pa.shape[1:]), blk(pb.shape[1:])],
        out_specs=blk((rows, w)),
        out_shape=jax.ShapeDtypeStruct((g, rows, w), F32),
        compiler_params=_cparams(("parallel",)),
        name="s5_scan",
    )(ug, tmat, bmat, cmat, pa, pb)


def _outproj_kernel(x_ref, ysb_ref, yscan_ref, u_ref, ymla_ref, d_ref, wglu_ref, mnw_ref, wout_ref, o_ref):
    mnw = mnw_ref[...]
    y = yscan_ref[...] + d_ref[...] * u_ref[...]
    g = jax.nn.gelu(y)
    gate = jax.nn.sigmoid(jnp.dot(g.astype(BF16), wglu_ref[...], preferred_element_type=F32))
    o_ssm = SB_WIDTH + SSM_WIDTH
    cat = jnp.concatenate([
        _rms(ysb_ref[...], mnw[:, :SB_WIDTH]).astype(BF16),
        _rms(g * gate, mnw[:, SB_WIDTH:o_ssm]).astype(BF16),
        _rms(ymla_ref[...], mnw[:, o_ssm:]).astype(BF16),
    ], axis=1)
    o_ref[...] = x_ref[...] + jnp.dot(cat, wout_ref[...], preferred_element_type=F32)


def _outproj(x, ysb, yscan, u, ymla, d, wglu, mnw, wout, *, tm):
    t, dm = x.shape
    row = lambda i: (i, 0)
    return pl.pallas_call(
        _outproj_kernel,
        grid=(t // tm,),
        in_specs=[
            pl.BlockSpec((tm, dm), row),
            pl.BlockSpec((tm, SB_WIDTH), row),
            pl.BlockSpec((tm, SSM_WIDTH), row),
            pl.BlockSpec((tm, SSM_WIDTH), row),
            pl.BlockSpec((tm, MLA_WIDTH), row),
            _const_spec((1, SSM_WIDTH)),
            _const_spec(wglu.shape),
            _const_spec((1, mnw.shape[1])),
            _const_spec(wout.shape),
        ],
        out_specs=pl.BlockSpec((tm, dm), row),
        out_shape=jax.ShapeDtypeStruct((t, dm), F32),
        compiler_params=_cparams(("parallel",)),
        name="outproj",
    )(x, ysb, yscan, u, ymla, d, wglu, mnw, wout)


FFN_TM = 512
FFN_TF = 512
PROJ_TM = 512
SB_TQ = 512
SB_TK = 256
MLA_TQ = 512


def kernel(x, norm_w, ffn_w13, ffn_w2, w_in, mla_q_norm_w, mla_w_uq, mla_kv_norm_w, mla_w_ukv,
           ssm_lam_re, ssm_lam_im, ssm_log_dt, ssm_b_re, ssm_b_im, ssm_c_re, ssm_c_im,
           ssm_d, ssm_w_glu, mix_norm_w, w_out, final_norm_w):
    batch, seq, dm = x.shape
    depth = norm_w.shape[0]
    t = batch * seq
    nchunk = seq // S5_L
    steps = nchunk.bit_length() - 1
    assert 1 << steps == nchunk
    tabc, tabs = _rope_tables(seq)
    umat = (lax.broadcasted_iota(jnp.int32, (SB_TK, SB_TK), 0)
            > lax.broadcasted_iota(jnp.int32, (SB_TK, SB_TK), 1)).astype(BF16)
    fw = final_norm_w[None, :]
    xt = x.reshape(t, dm)
    for l in range(depth):
        w13a, w2a = _prep_ffn(ffn_w13[l, 0], ffn_w2[l, 0], FFN_TF)
        xt = _ffn(xt, norm_w[l, 0][None], w13a, w2a, fw, tm=FFN_TM, tf=FFN_TF, final=False)

        winp, wuqp, wukv = _prep_inproj(w_in[l], mla_w_uq[l], mla_w_ukv[l])
        qkv, u, qc, kv, kr = _inproj(xt, norm_w[l, 1][None], winp, mla_q_norm_w[l][None], wuqp,
                                     mla_kv_norm_w[l][None], wukv, tabc, tabs, tm=PROJ_TM, seq=seq)
        ysb = _sb_attention(qkv, umat, batch=batch, seq=seq, tq=SB_TQ, tk=SB_TK)
        ymla = _mla_attention(qc, kv, kr, batch=batch, seq=seq, tq=MLA_TQ)

        tmat, bmat, cmat, pa, pb = _s5_prep(ssm_lam_re[l], ssm_lam_im[l], ssm_log_dt[l],
                                            ssm_b_re[l], ssm_b_im[l], ssm_c_re[l], ssm_c_im[l], steps=steps)
        ug = u.reshape(batch, nchunk, S5_L, SSM_GROUPS, SSM_GROUP).transpose(3, 0, 1, 2, 4)
        ug = ug.reshape(SSM_GROUPS, batch * nchunk, S5_W).astype(BF16)
        yg = _s5_scan(ug, tmat, bmat, cmat, pa, pb, nchunk=nchunk)
        yscan = yg.reshape(SSM_GROUPS, batch, nchunk, S5_L, SSM_GROUP).transpose(1, 2, 3, 0, 4)
        yscan = yscan.reshape(t, SSM_WIDTH)

        xt = _outproj(xt, ysb, yscan, u, ymla, ssm_d[l][None], ssm_w_glu[l].astype(BF16),
                      mix_norm_w[l][None], w_out[l].astype(BF16), tm=PROJ_TM)

        w13b, w2b = _prep_ffn(ffn_w13[l, 1], ffn_w2[l, 1], FFN_TF)
        xt = _ffn(xt, norm_w[l, 2][None], w13b, w2b, fw, tm=FFN_TM, tf=FFN_TF, final=(l == depth - 1))
    return xt.reshape(batch, seq, dm)
```

```python
import functools
import math

import jax
import jax.numpy as jnp
from jax import lax
from jax.experimental import pallas as pl
from jax.experimental.pallas import tpu as pltpu

F32 = jnp.float32
BF16 = jnp.bfloat16

EPS = 1e-6
NEG_INF = -1e30
LOG2E = math.log2(math.e)
LANE = 128
VMEM_LIMIT = 56 * 1024 * 1024

CHUNK = 64
SB_HEADS = 8
SB_HEAD_DIM = 64
SB_WIDTH = SB_HEADS * SB_HEAD_DIM
SSM_WIDTH = 512
SSM_GROUP = 16
SSM_GROUPS = SSM_WIDTH // SSM_GROUP
SSM_STATE = 64
MLA_HEADS = 8
MLA_NOPE = 128
MLA_ROPE = 64
MLA_V = 128
MLA_Q_RANK = 512
MLA_KV_RANK = 256
MLA_WIDTH = MLA_HEADS * MLA_V
ROPE_BASE = 10000.0
ROPE_HALF = MLA_ROPE // 2
MLA_QW = MLA_NOPE + LANE

S5_L = 16
S5_GPT = LANE // SSM_GROUP
S5_TILES = SSM_WIDTH // LANE
S5_W = S5_L * LANE
S5_SW = S5_GPT * 2 * SSM_STATE


def _cparams(sem):
    return pltpu.CompilerParams(dimension_semantics=sem, vmem_limit_bytes=VMEM_LIMIT)


def _rms(x, w):
    ms = jnp.mean(x * x, axis=-1, keepdims=True)
    return x * lax.rsqrt(ms + EPS) * w


def _const_spec(shape):
    nd = len(shape)
    return pl.BlockSpec(shape, lambda *_: (0,) * nd, pipeline_mode=pl.Buffered(1))


def _ffn_kernel(x_ref, nw_ref, w1_ref, w3_ref, w2_ref, fw_ref, o_ref, h_ref, *, final):
    j = pl.program_id(1)

    @pl.when(j == 0)
    def _():
        x = x_ref[...]
        h_ref[...] = _rms(x, nw_ref[...]).astype(BF16)
        o_ref[...] = x

    h = h_ref[...]
    a = jnp.dot(h, w1_ref[...], preferred_element_type=F32)
    b = jnp.dot(h, w3_ref[...], preferred_element_type=F32)
    g = (a * jax.nn.sigmoid(a)) * b * 0.5
    o_ref[...] += jnp.dot(g.astype(BF16), w2_ref[...], preferred_element_type=F32)

    if final:
        @pl.when(j == pl.num_programs(1) - 1)
        def _():
            o_ref[...] = _rms(o_ref[...], fw_ref[...])


def _ffn(x, nw, w13p, w2p, fw, *, tm, tf, final):
    t, d = x.shape
    nt = w2p.shape[0] // tf
    return pl.pallas_call(
        functools.partial(_ffn_kernel, final=final),
        grid=(t // tm, nt),
        in_specs=[
            pl.BlockSpec((tm, d), lambda i, j: (i, 0)),
            pl.BlockSpec((1, d), lambda i, j: (0, 0)),
            pl.BlockSpec((d, tf), lambda i, j: (0, j)),
            pl.BlockSpec((d, tf), lambda i, j: (0, nt + j)),
            pl.BlockSpec((tf, d), lambda i, j: (j, 0)),
            pl.BlockSpec((1, d), lambda i, j: (0, 0)),
        ],
        out_specs=pl.BlockSpec((tm, d), lambda i, j: (i, 0)),
        out_shape=jax.ShapeDtypeStruct((t, d), F32),
        scratch_shapes=[pltpu.VMEM((tm, d), BF16)],
        compiler_params=_cparams(("parallel", "arbitrary")),
        name="ffn",
    )(x, nw, w13p, w13p, w2p, fw)


def _prep_ffn(w13, w2, tf):
    d, two_ff = w13.shape
    ff = two_ff // 2
    pad = -(-ff // tf) * tf - ff
    w13p = jnp.pad(w13.astype(BF16).reshape(d, 2, ff), ((0, 0), (0, 0), (0, pad))).reshape(d, 2 * (ff + pad))
    w2p = jnp.pad(w2.astype(BF16), ((0, pad), (0, 0)))
    return w13p, w2p


def _rope(r, tc, ts):
    return r * tc + pltpu.roll(r, 2 * ROPE_HALF, 1) * ts


def _inproj_kernel(x_ref, nw_ref, win_ref, qnw_ref, wuq_ref, kvnw_ref, wukv_ref, tc_ref, ts_ref,
                   qkv_ref, u_ref, u4_ref, qc_ref, kv_ref, kr_ref):
    h = _rms(x_ref[...], nw_ref[...]).astype(BF16)
    y = jnp.dot(h, win_ref[...], preferred_element_type=F32)
    o_u = 3 * SB_WIDTH
    o_cq = o_u + SSM_WIDTH
    o_ckv = o_cq + MLA_Q_RANK
    o_kr = o_ckv + MLA_KV_RANK
    qkv_ref[:, :SB_WIDTH] = (y[:, :SB_WIDTH] * (SB_HEAD_DIM ** -0.5)).astype(BF16)
    qkv_ref[:, SB_WIDTH:] = y[:, SB_WIDTH:o_u].astype(BF16)
    u = y[:, o_u:o_cq]
    u_ref[...] = u
    for k in range(S5_TILES):
        u4_ref[k] = u[:, k * LANE:(k + 1) * LANE].astype(BF16)
    tc = tc_ref[...]
    ts = ts_ref[...]
    cq = _rms(y[:, o_cq:o_ckv], qnw_ref[...]).astype(BF16)
    q = jnp.dot(cq, wuq_ref[...], preferred_element_type=F32) * ((MLA_NOPE + MLA_ROPE) ** -0.5 * LOG2E)
    for hh in range(MLA_HEADS):
        o = hh * MLA_QW
        qc_ref[:, o:o + MLA_NOPE] = q[:, o:o + MLA_NOPE].astype(BF16)
        qc_ref[:, o + MLA_NOPE:o + MLA_QW] = _rope(q[:, o + MLA_NOPE:o + MLA_QW], tc, ts).astype(BF16)
    ckv = _rms(y[:, o_ckv:o_kr], kvnw_ref[...]).astype(BF16)
    kv_ref[...] = jnp.dot(ckv, wukv_ref[...], preferred_element_type=F32).astype(BF16)
    kr_ref[...] = _rope(y[:, o_kr:o_kr + LANE], tc, ts).astype(BF16)


def _inproj(x, nw, winp, qnw, wuqp, kvnw, wukv, tabc, tabs, *, tm, seq):
    t, d = x.shape
    nseq = seq // tm
    row = lambda i: (i, 0)
    return pl.pallas_call(
        _inproj_kernel,
        grid=(t // tm,),
        in_specs=[
            pl.BlockSpec((tm, d), row),
            _const_spec((1, d)),
            _const_spec(winp.shape),
            _const_spec((1, MLA_Q_RANK)),
            _const_spec(wuqp.shape),
            _const_spec((1, MLA_KV_RANK)),
            _const_spec(wukv.shape),
            pl.BlockSpec((tm, LANE), lambda i: (i % nseq, 0)),
            pl.BlockSpec((tm, LANE), lambda i: (i % nseq, 0)),
        ],
        out_specs=[
            pl.BlockSpec((tm, 3 * SB_WIDTH), row),
            pl.BlockSpec((tm, SSM_WIDTH), row),
            pl.BlockSpec((S5_TILES, tm, LANE), lambda i: (0, i, 0)),
            pl.BlockSpec((tm, MLA_HEADS * MLA_QW), row),
            pl.BlockSpec((tm, MLA_HEADS * (MLA_NOPE + MLA_V)), row),
            pl.BlockSpec((tm, LANE), row),
        ],
        out_shape=[
            jax.ShapeDtypeStruct((t, 3 * SB_WIDTH), BF16),
            jax.ShapeDtypeStruct((t, SSM_WIDTH), F32),
            jax.ShapeDtypeStruct((S5_TILES, t, LANE), BF16),
            jax.ShapeDtypeStruct((t, MLA_HEADS * MLA_QW), BF16),
            jax.ShapeDtypeStruct((t, MLA_HEADS * (MLA_NOPE + MLA_V)), BF16),
            jax.ShapeDtypeStruct((t, LANE), BF16),
        ],
        compiler_params=_cparams(("parallel",)),
        name="inproj",
    )(x, nw, winp, qnw, wuqp, kvnw, wukv, tabc, tabs)


def _rope_pad(w):
    z = jnp.zeros(w.shape[:-1] + (ROPE_HALF,), w.dtype)
    return jnp.concatenate([w[..., :ROPE_HALF], z, w[..., ROPE_HALF:], z], axis=-1)


def _prep_inproj(w_in, w_uq, w_ukv):
    o_kr = 3 * SB_WIDTH + SSM_WIDTH + MLA_Q_RANK + MLA_KV_RANK
    w_in = w_in.astype(BF16)
    winp = jnp.concatenate([w_in[:, :o_kr], _rope_pad(w_in[:, o_kr:])], axis=1)
    wq = w_uq.astype(BF16).reshape(MLA_Q_RANK, MLA_HEADS, MLA_NOPE + MLA_ROPE)
    wuqp = jnp.concatenate([wq[..., :MLA_NOPE], _rope_pad(wq[..., MLA_NOPE:])], axis=-1)
    wuqp = wuqp.reshape(MLA_Q_RANK, MLA_HEADS * MLA_QW)
    return winp, wuqp, w_ukv.astype(BF16)


def _rope_tables(seq):
    inv = ROPE_BASE ** (-jnp.arange(0, MLA_ROPE, 2, dtype=F32) / MLA_ROPE)
    ang = jnp.arange(seq, dtype=F32)[:, None] * inv[None, :]
    c, s = jnp.cos(ang), jnp.sin(ang)
    z = jnp.zeros_like(c)
    return jnp.concatenate([c, z, c, z], axis=1), jnp.concatenate([-s, z, s, z], axis=1)


def _sb_kernel(q_ref, k_ref, v_ref, u_ref, o_ref, z_s, lb_s, tl_s, rs_s, cr_s, acc_s, *, tq, tk):
    i = pl.program_id(2)
    nd = tq // tk
    assert nd == 2
    nblk = (i + 1) * nd
    first_head = lax.broadcasted_iota(jnp.int32, (1, LANE), 1) < SB_HEAD_DIM
    q = q_ref[...]
    zero = jnp.zeros_like(q)
    q2 = jnp.concatenate([jnp.where(first_head, q, zero), jnp.where(first_head, zero, q)], axis=0)
    umat = u_ref[...]
    rowpos = i * tq + lax.broadcasted_iota(jnp.int32, (2 * tq, 1), 0) % tq
    colofs = lax.broadcasted_iota(jnp.int32, (1, tk), 1)

    def start(t):
        return pl.multiple_of((nblk - 1 - t) * tk, tk)

    def scores(t):
        z_s[...] = lax.dot_general(q2, k_ref[pl.ds(start(t), tk), :], (((1,), (1,)), ((), ())),
                                   preferred_element_type=F32)

    def cumsum(t, masked):
        z = z_s[...]
        sp = jnp.maximum(z, 0.0) + jnp.log(1.0 + jnp.exp(-jnp.abs(z)))
        lb = z - sp
        if masked:
            strict = (start(t) + colofs) < rowpos
            sp = jnp.where(strict, sp, 0.0)
            lb = jnp.where(strict, lb, NEG_INF)
        lb_s[...] = lb
        rs_s[...] = jnp.sum(sp, axis=-1, keepdims=True)
        hi = sp.astype(BF16)
        lo = (sp - hi.astype(F32)).astype(BF16)
        tl_s[...] = (jnp.dot(hi, umat, preferred_element_type=F32)
                     + jnp.dot(lo, umat, preferred_element_type=F32))

    def weigh(t):
        carry = cr_s[...]
        w = jnp.exp(lb_s[...] - tl_s[...] - carry)
        acc_s[...] += jnp.dot(w.astype(BF16), v_ref[pl.ds(start(t), tk), :], preferred_element_type=F32)
        cr_s[...] = carry + rs_s[...]

    scores(0)
    cumsum(0, True)
    scores(1)
    cr_s[...] = jnp.zeros_like(cr_s)
    acc_s[...] = jnp.zeros_like(acc_s)

    @pl.when(i == 0)
    def _():
        weigh(0)
        cumsum(1, True)
        weigh(1)

    @pl.when(i > 0)
    def _():
        weigh(0)
        cumsum(1, True)
        scores(2)

        def body(t, c):
            weigh(t - 2)
            cumsum(t - 1, False)
            scores(t)
            return c

        lax.fori_loop(3, nblk, body, 0)
        weigh(nblk - 2)
        cumsum(nblk - 1, False)
        weigh(nblk - 1)

    acc = acc_s[...]
    o_ref[...] = jnp.where(first_head, acc[:tq], acc[tq:])


def _sb_attention(qkv, umat, *, batch, seq, tq, tk):
    t = qkv.shape[0]
    nq = seq // tq
    npair = SB_WIDTH // LANE
    return pl.pallas_call(
        functools.partial(_sb_kernel, tq=tq, tk=tk),
        grid=(batch, npair, nq),
        in_specs=[
            pl.BlockSpec((tq, LANE), lambda b, p, i: (b * nq + i, p)),
            pl.BlockSpec((seq, LANE), lambda b, p, i: (b, npair + p)),
            pl.BlockSpec((seq, LANE), lambda b, p, i: (b, 2 * npair + p)),
            pl.BlockSpec((tk, tk), lambda b, p, i: (0, 0)),
        ],
        out_specs=pl.BlockSpec((tq, LANE), lambda b, p, i: (b * nq + i, p)),
        out_shape=jax.ShapeDtypeStruct((t, SB_WIDTH), F32),
        scratch_shapes=[pltpu.VMEM((2 * tq, tk), F32), pltpu.VMEM((2 * tq, tk), F32), pltpu.VMEM((2 * tq, tk), F32),
                        pltpu.VMEM((2 * tq, 1), F32), pltpu.VMEM((2 * tq, 1), F32), pltpu.VMEM((2 * tq, LANE), F32)],
        compiler_params=_cparams(("parallel", "parallel", "arbitrary")),
        name="sb_attn",
    )(qkv, qkv, qkv, umat)


def _mla_kernel(q_ref, kn0_ref, v0_ref, kn1_ref, v1_ref, kr_ref, o_ref, s_s, m_s, l_s, acc_s, *, tq):
    i = pl.program_id(2)
    tk = tq
    q = q_ref[...]
    heads = ((q[:, :MLA_QW], kn0_ref, v0_ref), (q[:, MLA_QW:], kn1_ref, v1_ref))
    qchunk = (i * tq + lax.broadcasted_iota(jnp.int32, (tq, 1), 0)) // CHUNK
    colofs = lax.broadcasted_iota(jnp.int32, (1, tk), 1)

    def scores(kb, masked):
        ks = pl.multiple_of(kb * tk, tk)
        krb = kr_ref[pl.ds(ks, tk), :]
        for hh, (qh, kn_ref, _) in enumerate(heads):
            kcat = jnp.concatenate([kn_ref[pl.ds(ks, tk), :], krb], axis=1)
            s = lax.dot_general(qh, kcat, (((1,), (1,)), ((), ())), preferred_element_type=F32)
            if masked:
                s = jnp.where((ks + colofs) // CHUNK <= qchunk, s, NEG_INF)
            s_s[hh] = s

    def update(kb):
        ks = pl.multiple_of(kb * tk, tk)
        for hh, (_, _, v_ref) in enumerate(heads):
            s = s_s[hh]
            m = m_s[hh]
            m_new = jnp.maximum(m, jnp.max(s, axis=-1, keepdims=True))
            alpha = jnp.exp2(m - m_new)
            p = jnp.exp2(s - m_new)
            m_s[hh] = m_new
            l_s[hh] = alpha * l_s[hh] + jnp.sum(p, axis=-1, keepdims=True)
            acc_s[hh] = alpha * acc_s[hh] + jnp.dot(p.astype(BF16), v_ref[pl.ds(ks, tk), :],
                                                    preferred_element_type=F32)

    m_s[...] = jnp.full(m_s.shape, NEG_INF, F32)
    l_s[...] = jnp.zeros_like(l_s)
    acc_s[...] = jnp.zeros_like(acc_s)
    scores(i, True)

    def body(t, c):
        update(i - t)
        scores(i - 1 - t, False)
        return c

    lax.fori_loop(0, i, body, 0)
    update(0)
    o_ref[...] = jnp.concatenate([acc_s[0] / l_s[0], acc_s[1] / l_s[1]], axis=1)


def _mla_attention(qc, kv, kr, *, batch, seq, tq):
    t = qc.shape[0]
    nq = seq // tq
    kvspec = lambda col: pl.BlockSpec((seq, LANE), lambda b, p, i: (b, 4 * p + col))
    return pl.pallas_call(
        functools.partial(_mla_kernel, tq=tq),
        grid=(batch, MLA_HEADS // 2, nq),
        in_specs=[
            pl.BlockSpec((tq, 2 * MLA_QW), lambda b, p, i: (b * nq + i, p)),
            kvspec(0), kvspec(1), kvspec(2), kvspec(3),
            pl.BlockSpec((seq, LANE), lambda b, p, i: (b, 0)),
        ],
        out_specs=pl.BlockSpec((tq, 2 * MLA_V), lambda b, p, i: (b * nq + i, p)),
        out_shape=jax.ShapeDtypeStruct((t, MLA_WIDTH), F32),
        scratch_shapes=[pltpu.VMEM((2, tq, tq), F32), pltpu.VMEM((2, tq, 1), F32), pltpu.VMEM((2, tq, 1), F32),
                        pltpu.VMEM((2, tq, MLA_V), F32)],
        compiler_params=_cparams(("parallel", "parallel", "arbitrary")),
        name="mla_attn",
    )(qc, kv, kv, kv, kv, kr)


def _cmul(ar, ai, br, bi):
    return ar * br - ai * bi, ar * bi + ai * br


def _s5_prep_kernel(lrc_ref, lic_ref, dtc_ref, ctr_ref, cti_ref, lrr_ref, lir_ref, dtr_ref, btr_ref, bti_ref,
                    tmat_ref, bmat_ref, cmat_ref, pa_ref, pb_ref):
    col_g = lax.broadcasted_iota(jnp.int32, (1, LANE), 1) // SSM_GROUP
    row_g = lax.broadcasted_iota(jnp.int32, (LANE, 1), 0) // SSM_GROUP
    dtc = jnp.exp(dtc_ref[0])
    ar = lrc_ref[0] * dtc
    ai = lic_ref[0] * dtc
    lr, li = jnp.exp(ar) * jnp.cos(ai), jnp.exp(ar) * jnp.sin(ai)
    er, ei = jnp.ones_like(lr), jnp.zeros_like(li)
    vmats = []
    for d in range(S5_L + 1):
        cr, ci = _cmul(ctr_ref[0], cti_ref[0], er, ei)
        vmats.append(jnp.concatenate([cr, -ci], axis=0))
        er, ei = _cmul(er, ei, lr, li)
    zero_blk = jnp.zeros((LANE, LANE), BF16)
    for g in range(S5_GPT):
        for i in range(S5_L):
            cmat_ref[0, g * LANE:(g + 1) * LANE, i * LANE:(i + 1) * LANE] = (
                jnp.where(col_g == g, vmats[i + 1], 0.0).astype(BF16))
    dtr = jnp.exp(dtr_ref[0])
    lam_r = lrr_ref[0]
    lam_i = lir_ref[0]
    arr = lam_r * dtr
    air = lam_i * dtr
    lbr, lbi = jnp.exp(arr) * jnp.cos(air), jnp.exp(arr) * jnp.sin(air)
    nr, ni = lbr - 1.0, lbi
    den = lam_r * lam_r + lam_i * lam_i
    qr = (nr * lam_r + ni * lam_i) / den
    qi = (ni * lam_r - nr * lam_i) / den
    is_re = lax.broadcasted_iota(jnp.int32, (1, 2 * SSM_STATE), 1) < SSM_STATE
    fr, fi = jnp.ones_like(lbr), jnp.zeros_like(lbi)
    bbar_t = None
    for j in range(S5_L - 1, -1, -1):
        wr, wi = _cmul(qr, qi, fr, fi)
        br, bi = _cmul(wr, wi, btr_ref[0], bti_ref[0])
        bj = jnp.where(is_re, br, bi)
        if j == S5_L - 1:
            bbar_t = bj
        for g in range(S5_GPT):
            bmat_ref[0, j * LANE:(j + 1) * LANE, g * LANE:(g + 1) * LANE] = (
                jnp.where(row_g == g, bj, 0.0).astype(BF16))
        fr, fi = _cmul(fr, fi, lbr, lbi)
    cexp = jnp.concatenate(vmats[:S5_L], axis=1)
    dall = jnp.dot(bbar_t, cexp, preferred_element_type=F32, precision=lax.Precision.HIGHEST)
    for d in range(S5_L):
        blk = jnp.where(row_g == col_g, dall[:, d * LANE:(d + 1) * LANE], 0.0).astype(BF16)
        for j in range(S5_L - d):
            i = j + d
            tmat_ref[0, j * LANE:(j + 1) * LANE, i * LANE:(i + 1) * LANE] = blk
    for j in range(1, S5_L):
        for i in range(j):
            tmat_ref[0, j * LANE:(j + 1) * LANE, i * LANE:(i + 1) * LANE] = zero_blk
    zr, zi = fr, fi
    for k in range(pa_ref.shape[1]):
        pa_ref[0, k] = zr
        pb_ref[0, k] = jnp.where(is_re, -zi, zi)
        zr, zi = _cmul(zr, zi, zr, zi)


def _s5_prep(lam_re, lam_im, log_dt, b_re, b_im, c_re, c_im, *, steps):
    g, p = lam_re.shape
    hg = SSM_GROUP
    col = lambda a: jnp.repeat(a.reshape(S5_TILES, S5_GPT, p).transpose(0, 2, 1), hg, axis=2)
    row = lambda a: jnp.tile(jnp.repeat(a.reshape(S5_TILES, S5_GPT, p), hg, axis=1), (1, 1, 2))
    ctc = lambda c: c.reshape(S5_TILES, S5_GPT, hg, p).transpose(0, 3, 1, 2).reshape(S5_TILES, p, LANE)
    btr = lambda b: jnp.tile(b.reshape(S5_TILES, S5_GPT, p, hg).transpose(0, 1, 3, 2).reshape(S5_TILES, LANE, p),
                             (1, 1, 2))
    dt_col = jnp.repeat(log_dt.reshape(S5_TILES, 1, S5_GPT), hg, axis=2)
    dt_row = jnp.repeat(log_dt.reshape(S5_TILES, S5_GPT, 1), hg, axis=1)
    blk = lambda *s: pl.BlockSpec((1,) + s, lambda i: (i,) + (0,) * len(s))
    return pl.pallas_call(
        _s5_prep_kernel,
        grid=(S5_TILES,),
        in_specs=[blk(p, LANE), blk(p, LANE), blk(1, LANE), blk(p, LANE), blk(p, LANE),
                  blk(LANE, 2 * p), blk(LANE, 2 * p), blk(LANE, 1), blk(LANE, 2 * p), blk(LANE, 2 * p)],
        out_specs=[blk(S5_W, S5_W), blk(S5_W, S5_SW), blk(S5_SW, S5_W),
                   blk(steps, LANE, 2 * p), blk(steps, LANE, 2 * p)],
        out_shape=[
            jax.ShapeDtypeStruct((S5_TILES, S5_W, S5_W), BF16),
            jax.ShapeDtypeStruct((S5_TILES, S5_W, S5_SW), BF16),
            jax.ShapeDtypeStruct((S5_TILES, S5_SW, S5_W), BF16),
            jax.ShapeDtypeStruct((S5_TILES, steps, LANE, 2 * p), F32),
            jax.ShapeDtypeStruct((S5_TILES, steps, LANE, 2 * p), F32),
        ],
        compiler_params=_cparams(("parallel",)),
        name="s5_prep",
    )(col(lam_re), col(lam_im), dt_col, ctc(c_re), ctc(c_im),
      row(lam_re), row(lam_im), dt_row, btr(b_re), btr(b_im))


def _s5_scan_kernel(u_ref, tmat_ref, bmat_ref, cmat_ref, pa_ref, pb_ref, y_ref, *, nchunk):
    u = u_ref[0]
    rows = u.shape[0]
    x = jnp.dot(u, bmat_ref[0], preferred_element_type=F32)
    cidx = lax.broadcasted_iota(jnp.int32, (rows, 1), 0) % nchunk
    steps = pa_ref.shape[1]
    s_in = []
    for g in range(S5_GPT):
        xg = x[:, g * 2 * SSM_STATE:(g + 1) * 2 * SSM_STATE]
        r0 = g * SSM_GROUP
        for k in range(steps):
            sh = 1 << k
            xs = jnp.where(cidx >= sh, pltpu.roll(xg, sh, 0), 0.0)
            xg = (xg + pa_ref[0, k, r0:r0 + 1, :] * xs
                  + pb_ref[0, k, r0:r0 + 1, :] * pltpu.roll(xs, SSM_STATE, 1))
        s_in.append(jnp.where(cidx >= 1, pltpu.roll(xg, 1, 0), 0.0).astype(BF16))
    s_in = jnp.concatenate(s_in, axis=1)
    y_ref[0] = jnp.dot(s_in, cmat_ref[0], preferred_element_type=F32)
    nq = 4
    qw = S5_W // nq
    for c in range(nq):
        y_ref[0, :, c * qw:(c + 1) * qw] += jnp.dot(u[:, :(c + 1) * qw], tmat_ref[0, :(c + 1) * qw, c * qw:(c + 1) * qw],
                                                    preferred_element_type=F32)


def _s5_scan(u4, tmat, bmat, cmat, pa, pb, *, nchunk, rows):
    nt, total, w = u4.shape
    return pl.pallas_call(
        functools.partial(_s5_scan_kernel, nchunk=nchunk),
        grid=(nt, total // rows),
        in_specs=[
            pl.BlockSpec((1, rows, w), lambda k, r: (k, r, 0)),
            pl.BlockSpec((1,) + tmat.shape[1:], lambda k, r: (k, 0, 0)),
            pl.BlockSpec((1,) + bmat.shape[1:], lambda k, r: (k, 0, 0)),
            pl.BlockSpec((1,) + cmat.shape[1:], lambda k, r: (k, 0, 0)),
            pl.BlockSpec((1,) + pa.shape[1:], lambda k, r: (k, 0, 0, 0)),
            pl.BlockSpec((1,) + pb.shape[1:], lambda k, r: (k, 0, 0, 0)),
        ],
        out_specs=pl.BlockSpec((1, rows, w), lambda k, r: (k, r, 0)),
        out_shape=jax.ShapeDtypeStruct((nt, total, w), F32),
        compiler_params=_cparams(("parallel", "parallel")),
        name="s5_scan",
    )(u4, tmat, bmat, cmat, pa, pb)


def _outproj_kernel(x_ref, ysb_ref, y4_ref, u_ref, ymla_ref, d_ref, wglu_ref, mnw_ref, wout_ref, o_ref):
    mnw = mnw_ref[...]
    yscan = jnp.concatenate([y4_ref[k] for k in range(S5_TILES)], axis=1)
    y = yscan + d_ref[...] * u_ref[...]
    g = jax.nn.gelu(y)
    gate = jax.nn.sigmoid(jnp.dot(g.astype(BF16), wglu_ref[...], preferred_element_type=F32))
    o_ssm = SB_WIDTH + SSM_WIDTH
    cat = jnp.concatenate([
        _rms(ysb_ref[...], mnw[:, :SB_WIDTH]).astype(BF16),
        _rms(g * gate, mnw[:, SB_WIDTH:o_ssm]).astype(BF16),
        _rms(ymla_ref[...], mnw[:, o_ssm:]).astype(BF16),
    ], axis=1)
    o_ref[...] = x_ref[...] + jnp.dot(cat, wout_ref[...], preferred_element_type=F32)


def _outproj(x, ysb, y4, u, ymla, d, wglu, mnw, wout, *, tm):
    t, dm = x.shape
    row = lambda i: (i, 0)
    return pl.pallas_call(
        _outproj_kernel,
        grid=(t // tm,),
        in_specs=[
            pl.BlockSpec((tm, dm), row),
            pl.BlockSpec((tm, SB_WIDTH), row),
            pl.BlockSpec((S5_TILES, tm, LANE), lambda i: (0, i, 0)),
            pl.BlockSpec((tm, SSM_WIDTH), row),
            pl.BlockSpec((tm, MLA_WIDTH), row),
            _const_spec((1, SSM_WIDTH)),
            _const_spec(wglu.shape),
            _const_spec((1, mnw.shape[1])),
            _const_spec(wout.shape),
        ],
        out_specs=pl.BlockSpec((tm, dm), row),
        out_shape=jax.ShapeDtypeStruct((t, dm), F32),
        compiler_params=_cparams(("parallel",)),
        name="outproj",
    )(x, ysb, y4, u, ymla, d, wglu, mnw, wout)


FFN_TM = 512
FFN_TF = 512
PROJ_TM = 512
SB_TQ = 512
SB_TK = 256
MLA_TQ = 512
S5_ROWS = 512


def kernel(x, norm_w, ffn_w13, ffn_w2, w_in, mla_q_norm_w, mla_w_uq, mla_kv_norm_w, mla_w_ukv,
           ssm_lam_re, ssm_lam_im, ssm_log_dt, ssm_b_re, ssm_b_im, ssm_c_re, ssm_c_im,
           ssm_d, ssm_w_glu, mix_norm_w, w_out, final_norm_w):
    batch, seq, dm = x.shape
    depth = norm_w.shape[0]
    t = batch * seq
    nchunk = seq // S5_L
    steps = nchunk.bit_length() - 1
    assert 1 << steps == nchunk
    s5_rows = min(S5_ROWS, batch * nchunk)
    assert s5_rows % nchunk == 0
    tabc, tabs = _rope_tables(seq)
    umat = (lax.broadcasted_iota(jnp.int32, (SB_TK, SB_TK), 0)
            > lax.broadcasted_iota(jnp.int32, (SB_TK, SB_TK), 1)).astype(BF16)
    fw = final_norm_w[None, :]
    xt = x.reshape(t, dm)
    for l in range(depth):
        w13a, w2a = _prep_ffn(ffn_w13[l, 0], ffn_w2[l, 0], FFN_TF)
        xt = _ffn(xt, norm_w[l, 0][None], w13a, w2a, fw, tm=FFN_TM, tf=FFN_TF, final=False)

        winp, wuqp, wukv = _prep_inproj(w_in[l], mla_w_uq[l], mla_w_ukv[l])
        qkv, u, u4, qc, kv, kr = _inproj(xt, norm_w[l, 1][None], winp, mla_q_norm_w[l][None], wuqp,
                                         mla_kv_norm_w[l][None], wukv, tabc, tabs, tm=PROJ_TM, seq=seq)
        ysb = _sb_attention(qkv, umat, batch=batch, seq=seq, tq=SB_TQ, tk=SB_TK)
        ymla = _mla_attention(qc, kv, kr, batch=batch, seq=seq, tq=MLA_TQ)

        tmat, bmat, cmat, pa, pb = _s5_prep(ssm_lam_re[l], ssm_lam_im[l], ssm_log_dt[l],
                                            ssm_b_re[l], ssm_b_im[l], ssm_c_re[l], ssm_c_im[l], steps=steps)
        y4 = _s5_scan(u4.reshape(S5_TILES, t // S5_L, S5_W), tmat, bmat, cmat, pa, pb,
                      nchunk=nchunk, rows=s5_rows)
        y4 = y4.reshape(S5_TILES, t, LANE)

        xt = _outproj(xt, ysb, y4, u, ymla, ssm_d[l][None], ssm_w_glu[l].astype(BF16),
                      mix_norm_w[l][None], w_out[l].astype(BF16), tm=PROJ_TM)

        w13b, w2b = _prep_ffn(ffn_w13[l, 1], ffn_w2[l, 1], FFN_TF)
        xt = _ffn(xt, norm_w[l, 2][None], w13b, w2b, fw, tm=FFN_TM, tf=FFN_TF, final=(l == depth - 1))
    return xt.reshape(batch, seq, dm)
```

```python
import functools
import math

import jax
import jax.numpy as jnp
from jax import lax
from jax.experimental import pallas as pl
from jax.experimental.pallas import tpu as pltpu

F32 = jnp.float32
BF16 = jnp.bfloat16

EPS = 1e-6
NEG_INF = -1e30
LOG2E = math.log2(math.e)
LANE = 128
VMEM_LIMIT = 56 * 1024 * 1024

CHUNK = 64
SB_HEADS = 8
SB_HEAD_DIM = 64
SB_WIDTH = SB_HEADS * SB_HEAD_DIM
SSM_WIDTH = 512
SSM_GROUP = 16
SSM_GROUPS = SSM_WIDTH // SSM_GROUP
SSM_STATE = 64
MLA_HEADS = 8
MLA_NOPE = 128
MLA_ROPE = 64
MLA_V = 128
MLA_Q_RANK = 512
MLA_KV_RANK = 256
MLA_WIDTH = MLA_HEADS * MLA_V
ROPE_BASE = 10000.0
ROPE_HALF = MLA_ROPE // 2
MLA_QW = MLA_NOPE + LANE

S5_L = 16
S5_GPT = LANE // SSM_GROUP
S5_TILES = SSM_WIDTH // LANE
S5_W = S5_L * LANE
S5_SW = S5_GPT * 2 * SSM_STATE


def _cparams(sem):
    return pltpu.CompilerParams(dimension_semantics=sem, vmem_limit_bytes=VMEM_LIMIT)


def _rms(x, w):
    ms = jnp.mean(x * x, axis=-1, keepdims=True)
    return x * lax.rsqrt(ms + EPS) * w


def _const_spec(shape):
    nd = len(shape)
    return pl.BlockSpec(shape, lambda *_: (0,) * nd, pipeline_mode=pl.Buffered(1))


def _ffn_kernel(x_ref, nw_ref, w1_ref, w3_ref, w2_ref, fw_ref, o_ref, h_ref, *, final):
    j = pl.program_id(1)

    @pl.when(j == 0)
    def _():
        x = x_ref[...]
        h_ref[...] = _rms(x, nw_ref[...]).astype(BF16)
        o_ref[...] = x

    h = h_ref[...]
    a = jnp.dot(h, w1_ref[...], preferred_element_type=F32)
    b = jnp.dot(h, w3_ref[...], preferred_element_type=F32)
    g = (a * jax.nn.sigmoid(a)) * b * 0.5
    o_ref[...] += jnp.dot(g.astype(BF16), w2_ref[...], preferred_element_type=F32)

    if final:
        @pl.when(j == pl.num_programs(1) - 1)
        def _():
            o_ref[...] = _rms(o_ref[...], fw_ref[...])


def _ffn(x, nw, w1p, w3p, w2p, fw, *, tm, tf, final):
    t, d = x.shape
    nt = w2p.shape[0] // tf
    return pl.pallas_call(
        functools.partial(_ffn_kernel, final=final),
        grid=(t // tm, nt),
        in_specs=[
            pl.BlockSpec((tm, d), lambda i, j: (i, 0)),
            pl.BlockSpec((1, d), lambda i, j: (0, 0)),
            pl.BlockSpec((d, tf), lambda i, j: (0, j)),
            pl.BlockSpec((d, tf), lambda i, j: (0, j)),
            pl.BlockSpec((tf, d), lambda i, j: (j, 0)),
            pl.BlockSpec((1, d), lambda i, j: (0, 0)),
        ],
        out_specs=pl.BlockSpec((tm, d), lambda i, j: (i, 0)),
        out_shape=jax.ShapeDtypeStruct((t, d), F32),
        scratch_shapes=[pltpu.VMEM((tm, d), BF16)],
        compiler_params=_cparams(("parallel", "arbitrary")),
        name="ffn",
    )(x, nw, w1p, w3p, w2p, fw)


def _prep_ffn(w13, w2, tf):
    ff = w13.shape[1] // 2
    pad = -(-ff // tf) * tf - ff
    w1p = jnp.pad(w13[:, :ff].astype(BF16), ((0, 0), (0, pad)))
    w3p = jnp.pad(w13[:, ff:].astype(BF16), ((0, 0), (0, pad)))
    w2p = jnp.pad(w2.astype(BF16), ((0, pad), (0, 0)))
    return w1p, w3p, w2p


def _rope(r, tc, ts):
    return r * tc + pltpu.roll(r, 2 * ROPE_HALF, 1) * ts


def _inproj_kernel(x_ref, nw_ref, win_ref, qnw_ref, wuq_ref, kvnw_ref, wukv_ref, tc_ref, ts_ref,
                   qkv_ref, u_ref, u4_ref, qc_ref, kv_ref, kr_ref):
    h = _rms(x_ref[...], nw_ref[...]).astype(BF16)
    y = jnp.dot(h, win_ref[...], preferred_element_type=F32)
    o_u = 3 * SB_WIDTH
    o_cq = o_u + SSM_WIDTH
    o_ckv = o_cq + MLA_Q_RANK
    o_kr = o_ckv + MLA_KV_RANK
    qkv_ref[:, :SB_WIDTH] = (y[:, :SB_WIDTH] * (SB_HEAD_DIM ** -0.5)).astype(BF16)
    qkv_ref[:, SB_WIDTH:] = y[:, SB_WIDTH:o_u].astype(BF16)
    u = y[:, o_u:o_cq]
    u_ref[...] = u
    for k in range(S5_TILES):
        u4_ref[k] = u[:, k * LANE:(k + 1) * LANE].astype(BF16)
    tc = tc_ref[...]
    ts = ts_ref[...]
    cq = _rms(y[:, o_cq:o_ckv], qnw_ref[...]).astype(BF16)
    q = jnp.dot(cq, wuq_ref[...], preferred_element_type=F32) * ((MLA_NOPE + MLA_ROPE) ** -0.5 * LOG2E)
    for hh in range(MLA_HEADS):
        o = hh * MLA_QW
        qc_ref[:, o:o + MLA_NOPE] = q[:, o:o + MLA_NOPE].astype(BF16)
        qc_ref[:, o + MLA_NOPE:o + MLA_QW] = _rope(q[:, o + MLA_NOPE:o + MLA_QW], tc, ts).astype(BF16)
    ckv = _rms(y[:, o_ckv:o_kr], kvnw_ref[...]).astype(BF16)
    kv_ref[...] = jnp.dot(ckv, wukv_ref[...], preferred_element_type=F32).astype(BF16)
    kr_ref[...] = _rope(y[:, o_kr:o_kr + LANE], tc, ts).astype(BF16)


def _inproj(x, nw, winp, qnw, wuqp, kvnw, wukv, tabc, tabs, *, tm, seq):
    t, d = x.shape
    nseq = seq // tm
    row = lambda i: (i, 0)
    return pl.pallas_call(
        _inproj_kernel,
        grid=(t // tm,),
        in_specs=[
            pl.BlockSpec((tm, d), row),
            _const_spec((1, d)),
            _const_spec(winp.shape),
            _const_spec((1, MLA_Q_RANK)),
            _const_spec(wuqp.shape),
            _const_spec((1, MLA_KV_RANK)),
            _const_spec(wukv.shape),
            pl.BlockSpec((tm, LANE), lambda i: (i % nseq, 0)),
            pl.BlockSpec((tm, LANE), lambda i: (i % nseq, 0)),
        ],
        out_specs=[
            pl.BlockSpec((tm, 3 * SB_WIDTH), row),
            pl.BlockSpec((tm, SSM_WIDTH), row),
            pl.BlockSpec((S5_TILES, tm, LANE), lambda i: (0, i, 0)),
            pl.BlockSpec((tm, MLA_HEADS * MLA_QW), row),
            pl.BlockSpec((tm, MLA_HEADS * (MLA_NOPE + MLA_V)), row),
            pl.BlockSpec((tm, LANE), row),
        ],
        out_shape=[
            jax.ShapeDtypeStruct((t, 3 * SB_WIDTH), BF16),
            jax.ShapeDtypeStruct((t, SSM_WIDTH), F32),
            jax.ShapeDtypeStruct((S5_TILES, t, LANE), BF16),
            jax.ShapeDtypeStruct((t, MLA_HEADS * MLA_QW), BF16),
            jax.ShapeDtypeStruct((t, MLA_HEADS * (MLA_NOPE + MLA_V)), BF16),
            jax.ShapeDtypeStruct((t, LANE), BF16),
        ],
        compiler_params=_cparams(("parallel",)),
        name="inproj",
    )(x, nw, winp, qnw, wuqp, kvnw, wukv, tabc, tabs)


def _rope_pad(w):
    z = jnp.zeros(w.shape[:-1] + (ROPE_HALF,), w.dtype)
    return jnp.concatenate([w[..., :ROPE_HALF], z, w[..., ROPE_HALF:], z], axis=-1)


def _prep_inproj(w_in, w_uq, w_ukv):
    o_kr = 3 * SB_WIDTH + SSM_WIDTH + MLA_Q_RANK + MLA_KV_RANK
    w_in = w_in.astype(BF16)
    winp = jnp.concatenate([w_in[:, :o_kr], _rope_pad(w_in[:, o_kr:])], axis=1)
    wq = w_uq.astype(BF16).reshape(MLA_Q_RANK, MLA_HEADS, MLA_NOPE + MLA_ROPE)
    wuqp = jnp.concatenate([wq[..., :MLA_NOPE], _rope_pad(wq[..., MLA_NOPE:])], axis=-1)
    wuqp = wuqp.reshape(MLA_Q_RANK, MLA_HEADS * MLA_QW)
    return winp, wuqp, w_ukv.astype(BF16)


def _rope_tables(seq):
    inv = ROPE_BASE ** (-jnp.arange(0, MLA_ROPE, 2, dtype=F32) / MLA_ROPE)
    ang = jnp.arange(seq, dtype=F32)[:, None] * inv[None, :]
    c, s = jnp.cos(ang), jnp.sin(ang)
    z = jnp.zeros_like(c)
    return jnp.concatenate([c, z, c, z], axis=1), jnp.concatenate([-s, z, s, z], axis=1)


def _sb_kernel(q_ref, k_ref, v_ref, u_ref, o_ref, z_s, lb_s, sp_s, cq_s, cb_s, w_s, acc_s, *, tq, tk):
    i = pl.program_id(2)
    nblk = (i + 1) * (tq // tk)
    assert tq // tk == 2
    first_head = lax.broadcasted_iota(jnp.int32, (1, LANE), 1) < SB_HEAD_DIM
    q = q_ref[...]
    zero = jnp.zeros_like(q)
    q2 = jnp.concatenate([jnp.where(first_head, q, zero), jnp.where(first_head, zero, q)], axis=0)
    umat = u_ref[...]
    rowpos = i * tq + lax.broadcasted_iota(jnp.int32, (2 * tq, 1), 0) % tq
    colofs = lax.broadcasted_iota(jnp.int32, (1, tk), 1)

    def start(t):
        return pl.multiple_of((nblk - 1 - t) * tk, tk)

    def s1(t):
        z_s[...] = lax.dot_general(q2, k_ref[pl.ds(start(t), tk), :], (((1,), (1,)), ((), ())),
                                   preferred_element_type=F32)

    def s2(t, masked):
        z = z_s[...]
        sp = jnp.maximum(z, 0.0) + jnp.log(1.0 + jnp.exp(-jnp.abs(z)))
        lb = z - sp
        if masked:
            strict = (start(t) + colofs) < rowpos
            sp = jnp.where(strict, sp, 0.0)
            lb = jnp.where(strict, lb, NEG_INF)
        lb_s[...] = lb
        sp_s[...] = sp.astype(BF16)
        carry = cb_s[...]
        cq_s[...] = carry
        cb_s[...] = carry + jnp.sum(sp, axis=-1, keepdims=True)

    def s3():
        tail = jnp.dot(sp_s[...], umat, preferred_element_type=F32)
        w_s[...] = jnp.exp(lb_s[...] - tail - cq_s[...]).astype(BF16)

    def s4(t):
        acc_s[...] += jnp.dot(w_s[...], v_ref[pl.ds(start(t), tk), :], preferred_element_type=F32)

    cb_s[...] = jnp.zeros_like(cb_s)
    acc_s[...] = jnp.zeros_like(acc_s)
    s1(0)
    s2(0, True)
    s1(1)
    s3()
    s2(1, True)

    @pl.when(i > 0)
    def _():
        s1(2)

        def body(t, c):
            s4(t - 3)
            s3()
            s2(t - 1, False)
            s1(t)
            return c

        lax.fori_loop(3, nblk, body, 0)
        s4(nblk - 3)
        s3()
        s2(nblk - 1, False)

    s4(nblk - 2)
    s3()
    s4(nblk - 1)
    acc = acc_s[...]
    o_ref[...] = jnp.where(first_head, acc[:tq], acc[tq:])


def _sb_attention(qkv, umat, *, batch, seq, tq, tk):
    t = qkv.shape[0]
    nq = seq // tq
    npair = SB_WIDTH // LANE
    return pl.pallas_call(
        functools.partial(_sb_kernel, tq=tq, tk=tk),
        grid=(batch, npair, nq),
        in_specs=[
            pl.BlockSpec((tq, LANE), lambda b, p, i: (b * nq + i, p)),
            pl.BlockSpec((seq, LANE), lambda b, p, i: (b, npair + p)),
            pl.BlockSpec((seq, LANE), lambda b, p, i: (b, 2 * npair + p)),
            pl.BlockSpec((tk, tk), lambda b, p, i: (0, 0)),
        ],
        out_specs=pl.BlockSpec((tq, LANE), lambda b, p, i: (b * nq + i, p)),
        out_shape=jax.ShapeDtypeStruct((t, SB_WIDTH), F32),
        scratch_shapes=[
            pltpu.VMEM((2 * tq, tk), F32),
            pltpu.VMEM((2 * tq, tk), F32),
            pltpu.VMEM((2 * tq, tk), BF16),
            pltpu.VMEM((2 * tq, 1), F32),
            pltpu.VMEM((2 * tq, 1), F32),
            pltpu.VMEM((2 * tq, tk), BF16),
            pltpu.VMEM((2 * tq, LANE), F32),
        ],
        compiler_params=_cparams(("parallel", "parallel", "arbitrary")),
        name="sb_attn",
    )(qkv, qkv, qkv, umat)


def _mla_kernel(q_ref, kn0_ref, v0_ref, kn1_ref, v1_ref, kr_ref, o_ref, s_s, p_s, al_s, m_s, l_s, acc_s, *, tq):
    i = pl.program_id(2)
    tk = tq
    q = q_ref[...]
    heads = ((q[:, :MLA_QW], kn0_ref, v0_ref), (q[:, MLA_QW:], kn1_ref, v1_ref))
    qchunk = (i * tq + lax.broadcasted_iota(jnp.int32, (tq, 1), 0)) // CHUNK
    colofs = lax.broadcasted_iota(jnp.int32, (1, tk), 1)

    def start(t):
        return pl.multiple_of((i - t) * tk, tk)

    def scores(t, masked):
        ks = start(t)
        krb = kr_ref[pl.ds(ks, tk), :]
        for hh, (qh, kn_ref, _) in enumerate(heads):
            kcat = jnp.concatenate([kn_ref[pl.ds(ks, tk), :], krb], axis=1)
            s = lax.dot_general(qh, kcat, (((1,), (1,)), ((), ())), preferred_element_type=F32)
            if masked:
                s = jnp.where((ks + colofs) // CHUNK <= qchunk, s, NEG_INF)
            s_s[hh] = s

    def softmax():
        for hh in range(2):
            s = s_s[hh]
            m = m_s[hh]
            m_new = jnp.maximum(m, jnp.max(s, axis=-1, keepdims=True))
            alpha = jnp.exp2(m - m_new)
            p = jnp.exp2(s - m_new)
            m_s[hh] = m_new
            l_s[hh] = alpha * l_s[hh] + jnp.sum(p, axis=-1, keepdims=True)
            al_s[hh] = alpha
            p_s[hh] = p.astype(BF16)

    def accumulate(t):
        for hh, (_, _, v_ref) in enumerate(heads):
            acc_s[hh] = al_s[hh] * acc_s[hh] + jnp.dot(p_s[hh], v_ref[pl.ds(start(t), tk), :],
                                                       preferred_element_type=F32)

    m_s[...] = jnp.full(m_s.shape, NEG_INF, F32)
    l_s[...] = jnp.zeros_like(l_s)
    acc_s[...] = jnp.zeros_like(acc_s)
    scores(0, True)
    softmax()

    @pl.when(i > 0)
    def _():
        scores(1, False)

        def body(t, c):
            accumulate(t - 2)
            softmax()
            scores(t, False)
            return c

        lax.fori_loop(2, i + 1, body, 0)
        accumulate(i - 1)
        softmax()

    accumulate(i)
    o_ref[...] = jnp.concatenate([acc_s[0] / l_s[0], acc_s[1] / l_s[1]], axis=1)


def _mla_attention(qc, kv, kr, *, batch, seq, tq):
    t = qc.shape[0]
    nq = seq // tq
    kvspec = lambda col: pl.BlockSpec((seq, LANE), lambda b, p, i: (b, 4 * p + col))
    return pl.pallas_call(
        functools.partial(_mla_kernel, tq=tq),
        grid=(batch, MLA_HEADS // 2, nq),
        in_specs=[
            pl.BlockSpec((tq, 2 * MLA_QW), lambda b, p, i: (b * nq + i, p)),
            kvspec(0), kvspec(1), kvspec(2), kvspec(3),
            pl.BlockSpec((seq, LANE), lambda b, p, i: (b, 0)),
        ],
        out_specs=pl.BlockSpec((tq, 2 * MLA_V), lambda b, p, i: (b * nq + i, p)),
        out_shape=jax.ShapeDtypeStruct((t, MLA_WIDTH), F32),
        scratch_shapes=[
            pltpu.VMEM((2, tq, tq), F32),
            pltpu.VMEM((2, tq, tq), BF16),
            pltpu.VMEM((2, tq, 1), F32),
            pltpu.VMEM((2, tq, 1), F32),
            pltpu.VMEM((2, tq, 1), F32),
            pltpu.VMEM((2, tq, MLA_V), F32),
        ],
        compiler_params=_cparams(("parallel", "parallel", "arbitrary")),
        name="mla_attn",
    )(qc, kv, kv, kv, kv, kr)


def _cmul(ar, ai, br, bi):
    return ar * br - ai * bi, ar * bi + ai * br


def _s5_prep_kernel(lrc_ref, lic_ref, dtc_ref, ctr_ref, cti_ref, lrr_ref, lir_ref, dtr_ref, btr_ref, bti_ref,
                    tmat_ref, bmat_ref, cmat_ref, pa_ref, pb_ref):
    col_g = lax.broadcasted_iota(jnp.int32, (1, LANE), 1) // SSM_GROUP
    row_g = lax.broadcasted_iota(jnp.int32, (LANE, 1), 0) // SSM_GROUP
    dtc = jnp.exp(dtc_ref[0])
    ar = lrc_ref[0] * dtc
    ai = lic_ref[0] * dtc
    lr, li = jnp.exp(ar) * jnp.cos(ai), jnp.exp(ar) * jnp.sin(ai)
    er, ei = jnp.ones_like(lr), jnp.zeros_like(li)
    vmats = []
    for d in range(S5_L + 1):
        cr, ci = _cmul(ctr_ref[0], cti_ref[0], er, ei)
        vmats.append(jnp.concatenate([cr, -ci], axis=0))
        er, ei = _cmul(er, ei, lr, li)
    zero_blk = jnp.zeros((LANE, LANE), BF16)
    for g in range(S5_GPT):
        for i in range(S5_L):
            cmat_ref[0, g * LANE:(g + 1) * LANE, i * LANE:(i + 1) * LANE] = (
                jnp.where(col_g == g, vmats[i + 1], 0.0).astype(BF16))
    dtr = jnp.exp(dtr_ref[0])
    lam_r = lrr_ref[0]
    lam_i = lir_ref[0]
    arr = lam_r * dtr
    air = lam_i * dtr
    lbr, lbi = jnp.exp(arr) * jnp.cos(air), jnp.exp(arr) * jnp.sin(air)
    nr, ni = lbr - 1.0, lbi
    den = lam_r * lam_r + lam_i * lam_i
    qr = (nr * lam_r + ni * lam_i) / den
    qi = (ni * lam_r - nr * lam_i) / den
    is_re = lax.broadcasted_iota(jnp.int32, (1, 2 * SSM_STATE), 1) < SSM_STATE
    fr, fi = jnp.ones_like(lbr), jnp.zeros_like(lbi)
    bbar_t = None
    for j in range(S5_L - 1, -1, -1):
        wr, wi = _cmul(qr, qi, fr, fi)
        br, bi = _cmul(wr, wi, btr_ref[0], bti_ref[0])
        bj = jnp.where(is_re, br, bi)
        if j == S5_L - 1:
            bbar_t = bj
        for g in range(S5_GPT):
            bmat_ref[0, j * LANE:(j + 1) * LANE, g * LANE:(g + 1) * LANE] = (
                jnp.where(row_g == g, bj, 0.0).astype(BF16))
        fr, fi = _cmul(fr, fi, lbr, lbi)
    cexp = jnp.concatenate(vmats[:S5_L], axis=1)
    dall = jnp.dot(bbar_t, cexp, preferred_element_type=F32, precision=lax.Precision.HIGHEST)
    for d in range(S5_L):
        blk = jnp.where(row_g == col_g, dall[:, d * LANE:(d + 1) * LANE], 0.0).astype(BF16)
        for j in range(S5_L - d):
            i = j + d
            tmat_ref[0, j * LANE:(j + 1) * LANE, i * LANE:(i + 1) * LANE] = blk
    for j in range(1, S5_L):
        for i in range(j):
            tmat_ref[0, j * LANE:(j + 1) * LANE, i * LANE:(i + 1) * LANE] = zero_blk
    zr, zi = fr, fi
    for k in range(pa_ref.shape[1]):
        pa_ref[0, k] = zr
        pb_ref[0, k] = jnp.where(is_re, -zi, zi)
        zr, zi = _cmul(zr, zi, zr, zi)


def _s5_prep(lam_re, lam_im, log_dt, b_re, b_im, c_re, c_im, *, steps):
    g, p = lam_re.shape
    hg = SSM_GROUP
    col = lambda a: jnp.repeat(a.reshape(S5_TILES, S5_GPT, p).transpose(0, 2, 1), hg, axis=2)
    row = lambda a: jnp.tile(jnp.repeat(a.reshape(S5_TILES, S5_GPT, p), hg, axis=1), (1, 1, 2))
    ctc = lambda c: c.reshape(S5_TILES, S5_GPT, hg, p).transpose(0, 3, 1, 2).reshape(S5_TILES, p, LANE)
    btr = lambda b: jnp.tile(b.reshape(S5_TILES, S5_GPT, p, hg).transpose(0, 1, 3, 2).reshape(S5_TILES, LANE, p),
                             (1, 1, 2))
    dt_col = jnp.repeat(log_dt.reshape(S5_TILES, 1, S5_GPT), hg, axis=2)
    dt_row = jnp.repeat(log_dt.reshape(S5_TILES, S5_GPT, 1), hg, axis=1)
    blk = lambda *s: pl.BlockSpec((1,) + s, lambda i: (i,) + (0,) * len(s))
    return pl.pallas_call(
        _s5_prep_kernel,
        grid=(S5_TILES,),
        in_specs=[blk(p, LANE), blk(p, LANE), blk(1, LANE), blk(p, LANE), blk(p, LANE),
                  blk(LANE, 2 * p), blk(LANE, 2 * p), blk(LANE, 1), blk(LANE, 2 * p), blk(LANE, 2 * p)],
        out_specs=[blk(S5_W, S5_W), blk(S5_W, S5_SW), blk(S5_SW, S5_W),
                   blk(steps, LANE, 2 * p), blk(steps, LANE, 2 * p)],
        out_shape=[
            jax.ShapeDtypeStruct((S5_TILES, S5_W, S5_W), BF16),
            jax.ShapeDtypeStruct((S5_TILES, S5_W, S5_SW), BF16),
            jax.ShapeDtypeStruct((S5_TILES, S5_SW, S5_W), BF16),
            jax.ShapeDtypeStruct((S5_TILES, steps, LANE, 2 * p), F32),
            jax.ShapeDtypeStruct((S5_TILES, steps, LANE, 2 * p), F32),
        ],
        compiler_params=_cparams(("parallel",)),
        name="s5_prep",
    )(col(lam_re), col(lam_im), dt_col, ctc(c_re), ctc(c_im),
      row(lam_re), row(lam_im), dt_row, btr(b_re), btr(b_im))


def _s5_scan_kernel(u_ref, tmat_ref, bmat_ref, cmat_ref, pa_ref, pb_ref, y_ref, *, nchunk):
    u = u_ref[0]
    rows = u.shape[0]
    x = jnp.dot(u, bmat_ref[0], preferred_element_type=F32)
    cidx = lax.broadcasted_iota(jnp.int32, (rows, 1), 0) % nchunk
    steps = pa_ref.shape[1]
    s_in = []
    for g in range(S5_GPT):
        xg = x[:, g * 2 * SSM_STATE:(g + 1) * 2 * SSM_STATE]
        r0 = g * SSM_GROUP
        for k in range(steps):
            sh = 1 << k
            xs = jnp.where(cidx >= sh, pltpu.roll(xg, sh, 0), 0.0)
            xg = (xg + pa_ref[0, k, r0:r0 + 1, :] * xs
                  + pb_ref[0, k, r0:r0 + 1, :] * pltpu.roll(xs, SSM_STATE, 1))
        s_in.append(jnp.where(cidx >= 1, pltpu.roll(xg, 1, 0), 0.0).astype(BF16))
    s_in = jnp.concatenate(s_in, axis=1)
    y_ref[0] = jnp.dot(s_in, cmat_ref[0], preferred_element_type=F32)
    nq = 4
    qw = S5_W // nq
    for c in range(nq):
        y_ref[0, :, c * qw:(c + 1) * qw] += jnp.dot(u[:, :(c + 1) * qw], tmat_ref[0, :(c + 1) * qw, c * qw:(c + 1) * qw],
                                                    preferred_element_type=F32)


def _s5_scan(u4, tmat, bmat, cmat, pa, pb, *, nchunk, rows):
    nt, total, w = u4.shape
    return pl.pallas_call(
        functools.partial(_s5_scan_kernel, nchunk=nchunk),
        grid=(nt, total // rows),
        in_specs=[
            pl.BlockSpec((1, rows, w), lambda k, r: (k, r, 0)),
            pl.BlockSpec((1,) + tmat.shape[1:], lambda k, r: (k, 0, 0)),
            pl.BlockSpec((1,) + bmat.shape[1:], lambda k, r: (k, 0, 0)),
            pl.BlockSpec((1,) + cmat.shape[1:], lambda k, r: (k, 0, 0)),
            pl.BlockSpec((1,) + pa.shape[1:], lambda k, r: (k, 0, 0, 0)),
            pl.BlockSpec((1,) + pb.shape[1:], lambda k, r: (k, 0, 0, 0)),
        ],
        out_specs=pl.BlockSpec((1, rows, w), lambda k, r: (k, r, 0)),
        out_shape=jax.ShapeDtypeStruct((nt, total, w), F32),
        compiler_params=_cparams(("parallel", "parallel")),
        name="s5_scan",
    )(u4, tmat, bmat, cmat, pa, pb)


def _outproj_kernel(x_ref, ysb_ref, y4_ref, u_ref, ymla_ref, d_ref, wglu_ref, mnw_ref, wout_ref, o_ref):
    mnw = mnw_ref[...]
    yscan = jnp.concatenate([y4_ref[k] for k in range(S5_TILES)], axis=1)
    y = yscan + d_ref[...] * u_ref[...]
    g = jax.nn.gelu(y)
    gate = jax.nn.sigmoid(jnp.dot(g.astype(BF16), wglu_ref[...], preferred_element_type=F32))
    o_ssm = SB_WIDTH + SSM_WIDTH
    cat = jnp.concatenate([
        _rms(ysb_ref[...], mnw[:, :SB_WIDTH]).astype(BF16),
        _rms(g * gate, mnw[:, SB_WIDTH:o_ssm]).astype(BF16),
        _rms(ymla_ref[...], mnw[:, o_ssm:]).astype(BF16),
    ], axis=1)
    o_ref[...] = x_ref[...] + jnp.dot(cat, wout_ref[...], preferred_element_type=F32)


def _outproj(x, ysb, y4, u, ymla, d, wglu, mnw, wout, *, tm):
    t, dm = x.shape
    row = lambda i: (i, 0)
    return pl.pallas_call(
        _outproj_kernel,
        grid=(t // tm,),
        in_specs=[
            pl.BlockSpec((tm, dm), row),
            pl.BlockSpec((tm, SB_WIDTH), row),
            pl.BlockSpec((S5_TILES, tm, LANE), lambda i: (0, i, 0)),
            pl.BlockSpec((tm, SSM_WIDTH), row),
            pl.BlockSpec((tm, MLA_WIDTH), row),
            _const_spec((1, SSM_WIDTH)),
            _const_spec(wglu.shape),
            _const_spec((1, mnw.shape[1])),
            _const_spec(wout.shape),
        ],
        out_specs=pl.BlockSpec((tm, dm), row),
        out_shape=jax.ShapeDtypeStruct((t, dm), F32),
        compiler_params=_cparams(("parallel",)),
        name="outproj",
    )(x, ysb, y4, u, ymla, d, wglu, mnw, wout)


FFN_TM = 512
FFN_TF = 512
PROJ_TM = 512
SB_TQ = 512
SB_TK = 256
MLA_TQ = 512
S5_ROWS = 512


def kernel(x, norm_w, ffn_w13, ffn_w2, w_in, mla_q_norm_w, mla_w_uq, mla_kv_norm_w, mla_w_ukv,
           ssm_lam_re, ssm_lam_im, ssm_log_dt, ssm_b_re, ssm_b_im, ssm_c_re, ssm_c_im,
           ssm_d, ssm_w_glu, mix_norm_w, w_out, final_norm_w):
    batch, seq, dm = x.shape
    depth = norm_w.shape[0]
    t = batch * seq
    nchunk = seq // S5_L
    steps = nchunk.bit_length() - 1
    assert 1 << steps == nchunk
    s5_rows = min(S5_ROWS, batch * nchunk)
    assert s5_rows % nchunk == 0
    tabc, tabs = _rope_tables(seq)
    umat = (lax.broadcasted_iota(jnp.int32, (SB_TK, SB_TK), 0)
            > lax.broadcasted_iota(jnp.int32, (SB_TK, SB_TK), 1)).astype(BF16)
    fw = final_norm_w[None, :]
    xt = x.reshape(t, dm)
    for l in range(depth):
        xt = _ffn(xt, norm_w[l, 0][None], *_prep_ffn(ffn_w13[l, 0], ffn_w2[l, 0], FFN_TF), fw,
                  tm=FFN_TM, tf=FFN_TF, final=False)

        winp, wuqp, wukv = _prep_inproj(w_in[l], mla_w_uq[l], mla_w_ukv[l])
        qkv, u, u4, qc, kv, kr = _inproj(xt, norm_w[l, 1][None], winp, mla_q_norm_w[l][None], wuqp,
                                         mla_kv_norm_w[l][None], wukv, tabc, tabs, tm=PROJ_TM, seq=seq)
        ysb = _sb_attention(qkv, umat, batch=batch, seq=seq, tq=SB_TQ, tk=SB_TK)
        ymla = _mla_attention(qc, kv, kr, batch=batch, seq=seq, tq=MLA_TQ)

        tmat, bmat, cmat, pa, pb = _s5_prep(ssm_lam_re[l], ssm_lam_im[l], ssm_log_dt[l],
                                            ssm_b_re[l], ssm_b_im[l], ssm_c_re[l], ssm_c_im[l], steps=steps)
        y4 = _s5_scan(u4.reshape(S5_TILES, t // S5_L, S5_W), tmat, bmat, cmat, pa, pb,
                      nchunk=nchunk, rows=s5_rows)
        y4 = y4.reshape(S5_TILES, t, LANE)

        xt = _outproj(xt, ysb, y4, u, ymla, ssm_d[l][None], ssm_w_glu[l].astype(BF16),
                      mix_norm_w[l][None], w_out[l].astype(BF16), tm=PROJ_TM)

        xt = _ffn(xt, norm_w[l, 2][None], *_prep_ffn(ffn_w13[l, 1], ffn_w2[l, 1], FFN_TF), fw,
                  tm=FFN_TM, tf=FFN_TF, final=(l == depth - 1))
    return xt.reshape(batch, seq, dm)
```

```python
import functools
import math

import jax
import jax.numpy as jnp
from jax import lax
from jax.experimental import pallas as pl
from jax.experimental.pallas import tpu as pltpu

F32 = jnp.float32
BF16 = jnp.bfloat16

EPS = 1e-6
NEG_INF = -1e30
LOG2E = math.log2(math.e)
LANE = 128
VMEM_LIMIT = 56 * 1024 * 1024

CHUNK = 64
SB_HEADS = 8
SB_HEAD_DIM = 64
SB_WIDTH = SB_HEADS * SB_HEAD_DIM
SSM_WIDTH = 512
SSM_GROUP = 16
SSM_GROUPS = SSM_WIDTH // SSM_GROUP
SSM_STATE = 64
MLA_HEADS = 8
MLA_NOPE = 128
MLA_ROPE = 64
MLA_V = 128
MLA_Q_RANK = 512
MLA_KV_RANK = 256
MLA_WIDTH = MLA_HEADS * MLA_V
ROPE_BASE = 10000.0
ROPE_HALF = MLA_ROPE // 2
MLA_QW = MLA_NOPE + LANE

S5_L = 16
S5_GPT = LANE // SSM_GROUP
S5_TILES = SSM_WIDTH // LANE
S5_W = S5_L * LANE
S5_SW = S5_GPT * 2 * SSM_STATE


def _cparams(sem):
    return pltpu.CompilerParams(dimension_semantics=sem, vmem_limit_bytes=VMEM_LIMIT)


def _rms(x, w):
    ms = jnp.mean(x * x, axis=-1, keepdims=True)
    return x * lax.rsqrt(ms + EPS) * w


def _const_spec(shape):
    nd = len(shape)
    return pl.BlockSpec(shape, lambda *_: (0,) * nd, pipeline_mode=pl.Buffered(1))


def _ffn_kernel(x_ref, nw_ref, w1_ref, w3_ref, w2_ref, fw_ref, o_ref, h_ref, *, final):
    j = pl.program_id(1)

    @pl.when(j == 0)
    def _():
        x = x_ref[...]
        h_ref[...] = _rms(x, nw_ref[...]).astype(BF16)
        o_ref[...] = x

    h = h_ref[...]
    a = jnp.dot(h, w1_ref[...], preferred_element_type=F32)
    b = jnp.dot(h, w3_ref[...], preferred_element_type=F32)
    g = (a * jax.nn.sigmoid(a)) * b * 0.5
    o_ref[...] += jnp.dot(g.astype(BF16), w2_ref[...], preferred_element_type=F32)

    if final:
        @pl.when(j == pl.num_programs(1) - 1)
        def _():
            o_ref[...] = _rms(o_ref[...], fw_ref[...])


def _ffn(x, nw, w1, w3, w2, fw, *, tm, tf, final):
    t, d = x.shape
    nt = w2.shape[0] // tf
    return pl.pallas_call(
        functools.partial(_ffn_kernel, final=final),
        grid=(t // tm, nt),
        in_specs=[
            pl.BlockSpec((tm, d), lambda i, j: (i, 0)),
            pl.BlockSpec((1, d), lambda i, j: (0, 0)),
            pl.BlockSpec((d, tf), lambda i, j: (0, j)),
            pl.BlockSpec((d, tf), lambda i, j: (0, j)),
            pl.BlockSpec((tf, d), lambda i, j: (j, 0)),
            pl.BlockSpec((1, d), lambda i, j: (0, 0)),
        ],
        out_specs=pl.BlockSpec((tm, d), lambda i, j: (i, 0)),
        out_shape=jax.ShapeDtypeStruct((t, d), F32),
        scratch_shapes=[pltpu.VMEM((tm, d), BF16)],
        compiler_params=_cparams(("parallel", "arbitrary")),
        name="ffn",
    )(x, nw, w1, w3, w2, fw)


def _prep_ffn(w13, w2, tf):
    ff = w13.shape[1] // 2
    pad = -(-ff // tf) * tf - ff
    w1p = jnp.pad(w13[:, :ff].astype(BF16), ((0, 0), (0, pad)))
    w3p = jnp.pad(w13[:, ff:].astype(BF16), ((0, 0), (0, pad)))
    w2p = jnp.pad(w2.astype(BF16), ((0, pad), (0, 0)))
    return w1p, w3p, w2p


def _rope(r, tc, ts):
    return r * tc + pltpu.roll(r, 2 * ROPE_HALF, 1) * ts


def _inproj_kernel(x_ref, nw_ref, win_ref, qnw_ref, wuq_ref, kvnw_ref, wukv_ref, tc_ref, ts_ref,
                   qkv_ref, u_ref, u4_ref, qc_ref, kv_ref, kr_ref):
    h = _rms(x_ref[...], nw_ref[...]).astype(BF16)
    y = jnp.dot(h, win_ref[...], preferred_element_type=F32)
    o_u = 3 * SB_WIDTH
    o_cq = o_u + SSM_WIDTH
    o_ckv = o_cq + MLA_Q_RANK
    o_kr = o_ckv + MLA_KV_RANK
    qkv_ref[:, :SB_WIDTH] = (y[:, :SB_WIDTH] * (SB_HEAD_DIM ** -0.5)).astype(BF16)
    qkv_ref[:, SB_WIDTH:] = y[:, SB_WIDTH:o_u].astype(BF16)
    u = y[:, o_u:o_cq]
    u_ref[...] = u
    for k in range(S5_TILES):
        u4_ref[k] = u[:, k * LANE:(k + 1) * LANE].astype(BF16)
    tc = tc_ref[...]
    ts = ts_ref[...]
    cq = _rms(y[:, o_cq:o_ckv], qnw_ref[...]).astype(BF16)
    q = jnp.dot(cq, wuq_ref[...], preferred_element_type=F32) * ((MLA_NOPE + MLA_ROPE) ** -0.5 * LOG2E)
    for hh in range(MLA_HEADS):
        o = hh * MLA_QW
        qc_ref[:, o:o + MLA_NOPE] = q[:, o:o + MLA_NOPE].astype(BF16)
        qc_ref[:, o + MLA_NOPE:o + MLA_QW] = _rope(q[:, o + MLA_NOPE:o + MLA_QW], tc, ts).astype(BF16)
    ckv = _rms(y[:, o_ckv:o_kr], kvnw_ref[...]).astype(BF16)
    kv_ref[...] = jnp.dot(ckv, wukv_ref[...], preferred_element_type=F32).astype(BF16)
    kr_ref[...] = _rope(y[:, o_kr:o_kr + LANE], tc, ts).astype(BF16)


def _inproj(x, nw, winp, qnw, wuqp, kvnw, wukv, tabc, tabs, *, tm, seq):
    t, d = x.shape
    nseq = seq // tm
    row = lambda i: (i, 0)
    return pl.pallas_call(
        _inproj_kernel,
        grid=(t // tm,),
        in_specs=[
            pl.BlockSpec((tm, d), row),
            _const_spec((1, d)),
            _const_spec(winp.shape),
            _const_spec((1, MLA_Q_RANK)),
            _const_spec(wuqp.shape),
            _const_spec((1, MLA_KV_RANK)),
            _const_spec(wukv.shape),
            pl.BlockSpec((tm, LANE), lambda i: (i % nseq, 0)),
            pl.BlockSpec((tm, LANE), lambda i: (i % nseq, 0)),
        ],
        out_specs=[
            pl.BlockSpec((tm, 3 * SB_WIDTH), row),
            pl.BlockSpec((tm, SSM_WIDTH), row),
            pl.BlockSpec((S5_TILES, tm, LANE), lambda i: (0, i, 0)),
            pl.BlockSpec((tm, MLA_HEADS * MLA_QW), row),
            pl.BlockSpec((tm, MLA_HEADS * (MLA_NOPE + MLA_V)), row),
            pl.BlockSpec((tm, LANE), row),
        ],
        out_shape=[
            jax.ShapeDtypeStruct((t, 3 * SB_WIDTH), BF16),
            jax.ShapeDtypeStruct((t, SSM_WIDTH), F32),
            jax.ShapeDtypeStruct((S5_TILES, t, LANE), BF16),
            jax.ShapeDtypeStruct((t, MLA_HEADS * MLA_QW), BF16),
            jax.ShapeDtypeStruct((t, MLA_HEADS * (MLA_NOPE + MLA_V)), BF16),
            jax.ShapeDtypeStruct((t, LANE), BF16),
        ],
        compiler_params=_cparams(("parallel",)),
        name="inproj",
    )(x, nw, winp, qnw, wuqp, kvnw, wukv, tabc, tabs)


def _rope_pad(w):
    z = jnp.zeros(w.shape[:-1] + (ROPE_HALF,), w.dtype)
    return jnp.concatenate([w[..., :ROPE_HALF], z, w[..., ROPE_HALF:], z], axis=-1)


def _prep_inproj(w_in, w_uq, w_ukv):
    o_kr = 3 * SB_WIDTH + SSM_WIDTH + MLA_Q_RANK + MLA_KV_RANK
    w_in = w_in.astype(BF16)
    winp = jnp.concatenate([w_in[:, :o_kr], _rope_pad(w_in[:, o_kr:])], axis=1)
    wq = w_uq.astype(BF16).reshape(MLA_Q_RANK, MLA_HEADS, MLA_NOPE + MLA_ROPE)
    wuqp = jnp.concatenate([wq[..., :MLA_NOPE], _rope_pad(wq[..., MLA_NOPE:])], axis=-1)
    wuqp = wuqp.reshape(MLA_Q_RANK, MLA_HEADS * MLA_QW)
    return winp, wuqp, w_ukv.astype(BF16)


def _rope_tables(seq):
    inv = ROPE_BASE ** (-jnp.arange(0, MLA_ROPE, 2, dtype=F32) / MLA_ROPE)
    ang = jnp.arange(seq, dtype=F32)[:, None] * inv[None, :]
    c, s = jnp.cos(ang), jnp.sin(ang)
    z = jnp.zeros_like(c)
    return jnp.concatenate([c, z, c, z], axis=1), jnp.concatenate([-s, z, s, z], axis=1)


def _sb_kernel(q_ref, k_ref, v_ref, u_ref, o_ref, z_s, lb_s, sp_s, cq_s, cb_s, w_s, acc_s, *, tq, tk):
    i = pl.program_id(2)
    nblk = (i + 1) * (tq // tk)
    assert tq // tk == 2
    first_head = lax.broadcasted_iota(jnp.int32, (1, LANE), 1) < SB_HEAD_DIM
    q = q_ref[...]
    zero = jnp.zeros_like(q)
    q2 = jnp.concatenate([jnp.where(first_head, q, zero), jnp.where(first_head, zero, q)], axis=0)
    umat = u_ref[...]
    rowpos = i * tq + lax.broadcasted_iota(jnp.int32, (2 * tq, 1), 0) % tq
    colofs = lax.broadcasted_iota(jnp.int32, (1, tk), 1)

    def start(t):
        return pl.multiple_of((nblk - 1 - t) * tk, tk)

    def s1(t):
        z_s[...] = lax.dot_general(q2, k_ref[pl.ds(start(t), tk), :], (((1,), (1,)), ((), ())),
                                   preferred_element_type=F32)

    def s2(t, masked):
        z = z_s[...]
        sp = jnp.maximum(z, 0.0) + jnp.log(1.0 + jnp.exp(-jnp.abs(z)))
        lb = z - sp
        if masked:
            strict = (start(t) + colofs) < rowpos
            sp = jnp.where(strict, sp, 0.0)
            lb = jnp.where(strict, lb, NEG_INF)
        lb_s[...] = lb
        sp_s[...] = sp.astype(BF16)
        carry = cb_s[...]
        cq_s[...] = carry
        cb_s[...] = carry + jnp.sum(sp, axis=-1, keepdims=True)

    def s3():
        tail = jnp.dot(sp_s[...], umat, preferred_element_type=F32)
        w_s[...] = jnp.exp(lb_s[...] - tail - cq_s[...]).astype(BF16)

    def s4(t):
        acc_s[...] += jnp.dot(w_s[...], v_ref[pl.ds(start(t), tk), :], preferred_element_type=F32)

    cb_s[...] = jnp.zeros_like(cb_s)
    acc_s[...] = jnp.zeros_like(acc_s)
    s1(0)
    s2(0, True)
    s1(1)
    s3()
    s2(1, True)

    @pl.when(i > 0)
    def _():
        s1(2)

        def body(t, c):
            s4(t - 3)
            s3()
            s2(t - 1, False)
            s1(t)
            return c

        lax.fori_loop(3, nblk, body, 0)
        s4(nblk - 3)
        s3()
        s2(nblk - 1, False)

    s4(nblk - 2)
    s3()
    s4(nblk - 1)
    acc = acc_s[...]
    o_ref[...] = jnp.where(first_head, acc[:tq], acc[tq:])


def _sb_attention(qkv, umat, *, batch, seq, tq, tk):
    t = qkv.shape[0]
    nq = seq // tq
    npair = SB_WIDTH // LANE
    return pl.pallas_call(
        functools.partial(_sb_kernel, tq=tq, tk=tk),
        grid=(batch, npair, nq),
        in_specs=[
            pl.BlockSpec((tq, LANE), lambda b, p, i: (b * nq + i, p)),
            pl.BlockSpec((seq, LANE), lambda b, p, i: (b, npair + p)),
            pl.BlockSpec((seq, LANE), lambda b, p, i: (b, 2 * npair + p)),
            pl.BlockSpec((tk, tk), lambda b, p, i: (0, 0)),
        ],
        out_specs=pl.BlockSpec((tq, LANE), lambda b, p, i: (b * nq + i, p)),
        out_shape=jax.ShapeDtypeStruct((t, SB_WIDTH), F32),
        scratch_shapes=[
            pltpu.VMEM((2 * tq, tk), F32),
            pltpu.VMEM((2 * tq, tk), F32),
            pltpu.VMEM((2 * tq, tk), BF16),
            pltpu.VMEM((2 * tq, 1), F32),
            pltpu.VMEM((2 * tq, 1), F32),
            pltpu.VMEM((2 * tq, tk), BF16),
            pltpu.VMEM((2 * tq, LANE), F32),
        ],
        compiler_params=_cparams(("parallel", "parallel", "arbitrary")),
        name="sb_attn",
    )(qkv, qkv, qkv, umat)


def _mla_kernel(q_ref, kn0_ref, v0_ref, kn1_ref, v1_ref, kr_ref, o_ref, s_s, m_s, l_s, acc_s, *, tq):
    i = pl.program_id(2)
    tk = tq
    q = q_ref[...]
    heads = ((q[:, :MLA_QW], kn0_ref, v0_ref), (q[:, MLA_QW:], kn1_ref, v1_ref))
    qchunk = (i * tq + lax.broadcasted_iota(jnp.int32, (tq, 1), 0)) // CHUNK
    colofs = lax.broadcasted_iota(jnp.int32, (1, tk), 1)

    def start(t):
        return pl.multiple_of((i - t) * tk, tk)

    def scores(t, masked):
        ks = start(t)
        krb = kr_ref[pl.ds(ks, tk), :]
        for hh, (qh, kn_ref, _) in enumerate(heads):
            kcat = jnp.concatenate([kn_ref[pl.ds(ks, tk), :], krb], axis=1)
            s = lax.dot_general(qh, kcat, (((1,), (1,)), ((), ())), preferred_element_type=F32)
            if masked:
                s = jnp.where((ks + colofs) // CHUNK <= qchunk, s, NEG_INF)
            s_s[hh] = s

    def update(t):
        for hh, (_, _, v_ref) in enumerate(heads):
            s = s_s[hh]
            m = m_s[hh]
            m_new = jnp.maximum(m, jnp.max(s, axis=-1, keepdims=True))
            alpha = jnp.exp2(m - m_new)
            p = jnp.exp2(s - m_new)
            m_s[hh] = m_new
            l_s[hh] = alpha * l_s[hh] + jnp.sum(p, axis=-1, keepdims=True)
            acc_s[hh] = alpha * acc_s[hh] + jnp.dot(p.astype(BF16), v_ref[pl.ds(start(t), tk), :],
                                                    preferred_element_type=F32)

    m_s[...] = jnp.full(m_s.shape, NEG_INF, F32)
    l_s[...] = jnp.zeros_like(l_s)
    acc_s[...] = jnp.zeros_like(acc_s)
    scores(0, True)

    def body(t, c):
        update(t - 1)
        scores(t, False)
        return c

    lax.fori_loop(1, i + 1, body, 0)
    update(i)
    o_ref[...] = jnp.concatenate([acc_s[0] / l_s[0], acc_s[1] / l_s[1]], axis=1)


def _mla_attention(qc, kv, kr, *, batch, seq, tq):
    t = qc.shape[0]
    nq = seq // tq
    kvspec = lambda col: pl.BlockSpec((seq, LANE), lambda b, p, i: (b, 4 * p + col))
    return pl.pallas_call(
        functools.partial(_mla_kernel, tq=tq),
        grid=(batch, MLA_HEADS // 2, nq),
        in_specs=[
            pl.BlockSpec((tq, 2 * MLA_QW), lambda b, p, i: (b * nq + i, p)),
            kvspec(0), kvspec(1), kvspec(2), kvspec(3),
            pl.BlockSpec((seq, LANE), lambda b, p, i: (b, 0)),
        ],
        out_specs=pl.BlockSpec((tq, 2 * MLA_V), lambda b, p, i: (b * nq + i, p)),
        out_shape=jax.ShapeDtypeStruct((t, MLA_WIDTH), F32),
        scratch_shapes=[
            pltpu.VMEM((2, tq, tq), F32),
            pltpu.VMEM((2, tq, 1), F32),
            pltpu.VMEM((2, tq, 1), F32),
            pltpu.VMEM((2, tq, MLA_V), F32),
        ],
        compiler_params=_cparams(("parallel", "parallel", "arbitrary")),
        name="mla_attn",
    )(qc, kv, kv, kv, kv, kr)


def _cmul(ar, ai, br, bi):
    return ar * br - ai * bi, ar * bi + ai * br


def _s5_prep_kernel(lrc_ref, lic_ref, dtc_ref, ctr_ref, cti_ref, lrr_ref, lir_ref, dtr_ref, btr_ref, bti_ref,
                    tmat_ref, bmat_ref, cmat_ref, pa_ref, pb_ref):
    col_g = lax.broadcasted_iota(jnp.int32, (1, LANE), 1) // SSM_GROUP
    row_g = lax.broadcasted_iota(jnp.int32, (LANE, 1), 0) // SSM_GROUP
    dtc = jnp.exp(dtc_ref[0])
    ar = lrc_ref[0] * dtc
    ai = lic_ref[0] * dtc
    lr, li = jnp.exp(ar) * jnp.cos(ai), jnp.exp(ar) * jnp.sin(ai)
    er, ei = jnp.ones_like(lr), jnp.zeros_like(li)
    vmats = []
    for d in range(S5_L + 1):
        cr, ci = _cmul(ctr_ref[0], cti_ref[0], er, ei)
        vmats.append(jnp.concatenate([cr, -ci], axis=0))
        er, ei = _cmul(er, ei, lr, li)
    zero_blk = jnp.zeros((LANE, LANE), BF16)
    for g in range(S5_GPT):
        for i in range(S5_L):
            cmat_ref[0, g * LANE:(g + 1) * LANE, i * LANE:(i + 1) * LANE] = (
                jnp.where(col_g == g, vmats[i + 1], 0.0).astype(BF16))
    dtr = jnp.exp(dtr_ref[0])
    lam_r = lrr_ref[0]
    lam_i = lir_ref[0]
    arr = lam_r * dtr
    air = lam_i * dtr
    lbr, lbi = jnp.exp(arr) * jnp.cos(air), jnp.exp(arr) * jnp.sin(air)
    nr, ni = lbr - 1.0, lbi
    den = lam_r * lam_r + lam_i * lam_i
    qr = (nr * lam_r + ni * lam_i) / den
    qi = (ni * lam_r - nr * lam_i) / den
    is_re = lax.broadcasted_iota(jnp.int32, (1, 2 * SSM_STATE), 1) < SSM_STATE
    fr, fi = jnp.ones_like(lbr), jnp.zeros_like(lbi)
    bbar_t = None
    for j in range(S5_L - 1, -1, -1):
        wr, wi = _cmul(qr, qi, fr, fi)
        br, bi = _cmul(wr, wi, btr_ref[0], bti_ref[0])
        bj = jnp.where(is_re, br, bi)
        if j == S5_L - 1:
            bbar_t = bj
        for g in range(S5_GPT):
            bmat_ref[0, j * LANE:(j + 1) * LANE, g * LANE:(g + 1) * LANE] = (
                jnp.where(row_g == g, bj, 0.0).astype(BF16))
        fr, fi = _cmul(fr, fi, lbr, lbi)
    cexp = jnp.concatenate(vmats[:S5_L], axis=1)
    dall = jnp.dot(bbar_t, cexp, preferred_element_type=F32, precision=lax.Precision.HIGHEST)
    for d in range(S5_L):
        blk = jnp.where(row_g == col_g, dall[:, d * LANE:(d + 1) * LANE], 0.0).astype(BF16)
        for j in range(S5_L - d):
            i = j + d
            tmat_ref[0, j * LANE:(j + 1) * LANE, i * LANE:(i + 1) * LANE] = blk
    for j in range(1, S5_L):
        for i in range(j):
            tmat_ref[0, j * LANE:(j + 1) * LANE, i * LANE:(i + 1) * LANE] = zero_blk
    zr, zi = fr, fi
    for k in range(pa_ref.shape[1]):
        pa_ref[0, k] = zr
        pb_ref[0, k] = jnp.where(is_re, -zi, zi)
        zr, zi = _cmul(zr, zi, zr, zi)


def _s5_prep(lam_re, lam_im, log_dt, b_re, b_im, c_re, c_im, *, steps):
    g, p = lam_re.shape
    hg = SSM_GROUP
    col = lambda a: jnp.repeat(a.reshape(S5_TILES, S5_GPT, p).transpose(0, 2, 1), hg, axis=2)
    row = lambda a: jnp.tile(jnp.repeat(a.reshape(S5_TILES, S5_GPT, p), hg, axis=1), (1, 1, 2))
    ctc = lambda c: c.reshape(S5_TILES, S5_GPT, hg, p).transpose(0, 3, 1, 2).reshape(S5_TILES, p, LANE)
    btr = lambda b: jnp.tile(b.reshape(S5_TILES, S5_GPT, p, hg).transpose(0, 1, 3, 2).reshape(S5_TILES, LANE, p),
                             (1, 1, 2))
    dt_col = jnp.repeat(log_dt.reshape(S5_TILES, 1, S5_GPT), hg, axis=2)
    dt_row = jnp.repeat(log_dt.reshape(S5_TILES, S5_GPT, 1), hg, axis=1)
    blk = lambda *s: pl.BlockSpec((1,) + s, lambda i: (i,) + (0,) * len(s))
    return pl.pallas_call(
        _s5_prep_kernel,
        grid=(S5_TILES,),
        in_specs=[blk(p, LANE), blk(p, LANE), blk(1, LANE), blk(p, LANE), blk(p, LANE),
                  blk(LANE, 2 * p), blk(LANE, 2 * p), blk(LANE, 1), blk(LANE, 2 * p), blk(LANE, 2 * p)],
        out_specs=[blk(S5_W, S5_W), blk(S5_W, S5_SW), blk(S5_SW, S5_W),
                   blk(steps, LANE, 2 * p), blk(steps, LANE, 2 * p)],
        out_shape=[
            jax.ShapeDtypeStruct((S5_TILES, S5_W, S5_W), BF16),
            jax.ShapeDtypeStruct((S5_TILES, S5_W, S5_SW), BF16),
            jax.ShapeDtypeStruct((S5_TILES, S5_SW, S5_W), BF16),
            jax.ShapeDtypeStruct((S5_TILES, steps, LANE, 2 * p), F32),
            jax.ShapeDtypeStruct((S5_TILES, steps, LANE, 2 * p), F32),
        ],
        compiler_params=_cparams(("parallel",)),
        name="s5_prep",
    )(col(lam_re), col(lam_im), dt_col, ctc(c_re), ctc(c_im),
      row(lam_re), row(lam_im), dt_row, btr(b_re), btr(b_im))


def _s5_scan_kernel(u_ref, tmat_ref, bmat_ref, cmat_ref, pa_ref, pb_ref, y_ref, *, nchunk):
    u = u_ref[0]
    rows = u.shape[0]
    x = jnp.dot(u, bmat_ref[0], preferred_element_type=F32)
    cidx = lax.broadcasted_iota(jnp.int32, (rows, 1), 0) % nchunk
    steps = pa_ref.shape[1]
    s_in = []
    for g in range(S5_GPT):
        xg = x[:, g * 2 * SSM_STATE:(g + 1) * 2 * SSM_STATE]
        r0 = g * SSM_GROUP
        for k in range(steps):
            sh = 1 << k
            xs = jnp.where(cidx >= sh, pltpu.roll(xg, sh, 0), 0.0)
            xg = (xg + pa_ref[0, k, r0:r0 + 1, :] * xs
                  + pb_ref[0, k, r0:r0 + 1, :] * pltpu.roll(xs, SSM_STATE, 1))
        s_in.append(jnp.where(cidx >= 1, pltpu.roll(xg, 1, 0), 0.0).astype(BF16))
    s_in = jnp.concatenate(s_in, axis=1)
    y_ref[0] = jnp.dot(s_in, cmat_ref[0], preferred_element_type=F32)
    nq = 4
    qw = S5_W // nq
    for c in range(nq):
        y_ref[0, :, c * qw:(c + 1) * qw] += jnp.dot(u[:, :(c + 1) * qw], tmat_ref[0, :(c + 1) * qw, c * qw:(c + 1) * qw],
                                                    preferred_element_type=F32)


def _s5_scan(u4, tmat, bmat, cmat, pa, pb, *, nchunk, rows):
    nt, total, w = u4.shape
    return pl.pallas_call(
        functools.partial(_s5_scan_kernel, nchunk=nchunk),
        grid=(nt, total // rows),
        in_specs=[
            pl.BlockSpec((1, rows, w), lambda k, r: (k, r, 0)),
            pl.BlockSpec((1,) + tmat.shape[1:], lambda k, r: (k, 0, 0)),
            pl.BlockSpec((1,) + bmat.shape[1:], lambda k, r: (k, 0, 0)),
            pl.BlockSpec((1,) + cmat.shape[1:], lambda k, r: (k, 0, 0)),
            pl.BlockSpec((1,) + pa.shape[1:], lambda k, r: (k, 0, 0, 0)),
            pl.BlockSpec((1,) + pb.shape[1:], lambda k, r: (k, 0, 0, 0)),
        ],
        out_specs=pl.BlockSpec((1, rows, w), lambda k, r: (k, r, 0)),
        out_shape=jax.ShapeDtypeStruct((nt, total, w), F32),
        compiler_params=_cparams(("parallel", "parallel")),
        name="s5_scan",
    )(u4, tmat, bmat, cmat, pa, pb)


def _outproj_kernel(x_ref, ysb_ref, y4_ref, u_ref, ymla_ref, d_ref, wglu_ref, mnw_ref, wout_ref, o_ref):
    mnw = mnw_ref[...]
    yscan = jnp.concatenate([y4_ref[k] for k in range(S5_TILES)], axis=1)
    y = yscan + d_ref[...] * u_ref[...]
    g = jax.nn.gelu(y)
    gate = jax.nn.sigmoid(jnp.dot(g.astype(BF16), wglu_ref[...], preferred_element_type=F32))
    o_ssm = SB_WIDTH + SSM_WIDTH
    cat = jnp.concatenate([
        _rms(ysb_ref[...], mnw[:, :SB_WIDTH]).astype(BF16),
        _rms(g * gate, mnw[:, SB_WIDTH:o_ssm]).astype(BF16),
        _rms(ymla_ref[...], mnw[:, o_ssm:]).astype(BF16),
    ], axis=1)
    o_ref[...] = x_ref[...] + jnp.dot(cat, wout_ref[...], preferred_element_type=F32)


def _outproj(x, ysb, y4, u, ymla, d, wglu, mnw, wout, *, tm):
    t, dm = x.shape
    row = lambda i: (i, 0)
    return pl.pallas_call(
        _outproj_kernel,
        grid=(t // tm,),
        in_specs=[
            pl.BlockSpec((tm, dm), row),
            pl.BlockSpec((tm, SB_WIDTH), row),
            pl.BlockSpec((S5_TILES, tm, LANE), lambda i: (0, i, 0)),
            pl.BlockSpec((tm, SSM_WIDTH), row),
            pl.BlockSpec((tm, MLA_WIDTH), row),
            _const_spec((1, SSM_WIDTH)),
            _const_spec(wglu.shape),
            _const_spec((1, mnw.shape[1])),
            _const_spec(wout.shape),
        ],
        out_specs=pl.BlockSpec((tm, dm), row),
        out_shape=jax.ShapeDtypeStruct((t, dm), F32),
        compiler_params=_cparams(("parallel",)),
        name="outproj",
    )(x, ysb, y4, u, ymla, d, wglu, mnw, wout)


FFN_TM = 1024
FFN_TF = 512
PROJ_TM = 512
SB_TQ = 512
SB_TK = 256
MLA_TQ = 512
S5_ROWS = 512


def kernel(x, norm_w, ffn_w13, ffn_w2, w_in, mla_q_norm_w, mla_w_uq, mla_kv_norm_w, mla_w_ukv,
           ssm_lam_re, ssm_lam_im, ssm_log_dt, ssm_b_re, ssm_b_im, ssm_c_re, ssm_c_im,
           ssm_d, ssm_w_glu, mix_norm_w, w_out, final_norm_w):
    batch, seq, dm = x.shape
    depth = norm_w.shape[0]
    t = batch * seq
    nchunk = seq // S5_L
    steps = nchunk.bit_length() - 1
    assert 1 << steps == nchunk
    s5_rows = min(S5_ROWS, batch * nchunk)
    assert s5_rows % nchunk == 0
    tabc, tabs = _rope_tables(seq)
    umat = (lax.broadcasted_iota(jnp.int32, (SB_TK, SB_TK), 0)
            > lax.broadcasted_iota(jnp.int32, (SB_TK, SB_TK), 1)).astype(BF16)
    fw = final_norm_w[None, :]
    xt = x.reshape(t, dm)
    for l in range(depth):
        xt = _ffn(xt, norm_w[l, 0][None], *_prep_ffn(ffn_w13[l, 0], ffn_w2[l, 0], FFN_TF), fw,
                  tm=FFN_TM, tf=FFN_TF, final=False)

        winp, wuqp, wukv = _prep_inproj(w_in[l], mla_w_uq[l], mla_w_ukv[l])
        qkv, u, u4, qc, kv, kr = _inproj(xt, norm_w[l, 1][None], winp, mla_q_norm_w[l][None], wuqp,
                                         mla_kv_norm_w[l][None], wukv, tabc, tabs, tm=PROJ_TM, seq=seq)
        ysb = _sb_attention(qkv, umat, batch=batch, seq=seq, tq=SB_TQ, tk=SB_TK)
        ymla = _mla_attention(qc, kv, kr, batch=batch, seq=seq, tq=MLA_TQ)

        tmat, bmat, cmat, pa, pb = _s5_prep(ssm_lam_re[l], ssm_lam_im[l], ssm_log_dt[l],
                                            ssm_b_re[l], ssm_b_im[l], ssm_c_re[l], ssm_c_im[l], steps=steps)
        y4 = _s5_scan(u4.reshape(S5_TILES, t // S5_L, S5_W), tmat, bmat, cmat, pa, pb,
                      nchunk=nchunk, rows=s5_rows)
        y4 = y4.reshape(S5_TILES, t, LANE)

        xt = _outproj(xt, ysb, y4, u, ymla, ssm_d[l][None], ssm_w_glu[l].astype(BF16),
                      mix_norm_w[l][None], w_out[l].astype(BF16), tm=PROJ_TM)

        xt = _ffn(xt, norm_w[l, 2][None], *_prep_ffn(ffn_w13[l, 1], ffn_w2[l, 1], FFN_TF), fw,
                  tm=FFN_TM, tf=FFN_TF, final=(l == depth - 1))
    return xt.reshape(batch, seq, dm)
```

```python
import functools
import math

import jax
import jax.numpy as jnp
from jax import lax
from jax.experimental import pallas as pl
from jax.experimental.pallas import tpu as pltpu

F32 = jnp.float32
BF16 = jnp.bfloat16

EPS = 1e-6
NEG_INF = -1e30
LOG2E = math.log2(math.e)
LANE = 128
VMEM_LIMIT = 56 * 1024 * 1024

CHUNK = 64
SB_HEADS = 8
SB_HEAD_DIM = 64
SB_WIDTH = SB_HEADS * SB_HEAD_DIM
SSM_WIDTH = 512
SSM_GROUP = 16
SSM_GROUPS = SSM_WIDTH // SSM_GROUP
SSM_STATE = 64
MLA_HEADS = 8
MLA_NOPE = 128
MLA_ROPE = 64
MLA_V = 128
MLA_Q_RANK = 512
MLA_KV_RANK = 256
MLA_WIDTH = MLA_HEADS * MLA_V
ROPE_BASE = 10000.0
ROPE_HALF = MLA_ROPE // 2
MLA_QW = MLA_NOPE + LANE

S5_L = 16
S5_GPT = LANE // SSM_GROUP
S5_TILES = SSM_WIDTH // LANE
S5_W = S5_L * LANE
S5_SW = S5_GPT * 2 * SSM_STATE


def _cparams(sem):
    return pltpu.CompilerParams(dimension_semantics=sem, vmem_limit_bytes=VMEM_LIMIT)


def _rms(x, w):
    ms = jnp.mean(x * x, axis=-1, keepdims=True)
    return x * lax.rsqrt(ms + EPS) * w


def _const_spec(shape):
    nd = len(shape)
    return pl.BlockSpec(shape, lambda *_: (0,) * nd, pipeline_mode=pl.Buffered(1))


def _ffn_kernel(x_ref, nw_ref, w1_ref, w3_ref, w2_ref, fw_ref, o_ref, h_ref, *, last, final):
    j = pl.program_id(1)
    nt = pl.num_programs(1)

    @pl.when(j == 0)
    def _():
        x = x_ref[...]
        h_ref[...] = _rms(x, nw_ref[...]).astype(BF16)
        o_ref[...] = x

    def tile(width):
        h = h_ref[...]
        a = jnp.dot(h, w1_ref[:, :width], preferred_element_type=F32)
        b = jnp.dot(h, w3_ref[:, :width], preferred_element_type=F32)
        g = (a * jax.nn.sigmoid(a)) * b * 0.5
        o_ref[...] += jnp.dot(g.astype(BF16), w2_ref[:width, :], preferred_element_type=F32)

    tf = w1_ref.shape[1]
    if last == tf:
        tile(tf)
    else:
        pl.when(j < nt - 1)(lambda: tile(tf))
        pl.when(j == nt - 1)(lambda: tile(last))

    if final:
        @pl.when(j == nt - 1)
        def _():
            o_ref[...] = _rms(o_ref[...], fw_ref[...])


def _ffn(x, nw, w1, w3, w2, fw, *, tm, tf, final):
    t, d = x.shape
    ff = w2.shape[0]
    nt = pl.cdiv(ff, tf)
    return pl.pallas_call(
        functools.partial(_ffn_kernel, last=ff - (nt - 1) * tf, final=final),
        grid=(t // tm, nt),
        in_specs=[
            pl.BlockSpec((tm, d), lambda i, j: (i, 0)),
            pl.BlockSpec((1, d), lambda i, j: (0, 0)),
            pl.BlockSpec((d, tf), lambda i, j: (0, j)),
            pl.BlockSpec((d, tf), lambda i, j: (0, j)),
            pl.BlockSpec((tf, d), lambda i, j: (j, 0)),
            pl.BlockSpec((1, d), lambda i, j: (0, 0)),
        ],
        out_specs=pl.BlockSpec((tm, d), lambda i, j: (i, 0)),
        out_shape=jax.ShapeDtypeStruct((t, d), F32),
        scratch_shapes=[pltpu.VMEM((tm, d), BF16)],
        compiler_params=_cparams(("parallel", "arbitrary")),
        name="ffn",
    )(x, nw, w1, w3, w2, fw)


def _prep_ffn(w13, w2):
    ff = w13.shape[1] // 2
    return w13[:, :ff].astype(BF16), w13[:, ff:].astype(BF16), w2.astype(BF16)


def _rope(r, tc, ts):
    return r * tc + pltpu.roll(r, 2 * ROPE_HALF, 1) * ts


def _inproj_kernel(x_ref, nw_ref, win_ref, qnw_ref, wuq_ref, kvnw_ref, wukv_ref, tc_ref, ts_ref,
                   qkv_ref, u_ref, u4_ref, qc_ref, kv_ref, kr_ref, us_ref):
    h = _rms(x_ref[...], nw_ref[...]).astype(BF16)
    y = jnp.dot(h, win_ref[...], preferred_element_type=F32)
    o_u = 3 * SB_WIDTH
    o_cq = o_u + SSM_WIDTH
    o_ckv = o_cq + MLA_Q_RANK
    o_kr = o_ckv + MLA_KV_RANK
    qkv_ref[:, :SB_WIDTH] = (y[:, :SB_WIDTH] * (SB_HEAD_DIM ** -0.5)).astype(BF16)
    qkv_ref[:, SB_WIDTH:] = y[:, SB_WIDTH:o_u].astype(BF16)
    u = y[:, o_u:o_cq]
    u_ref[...] = u
    nrow = u.shape[0] // S5_L
    for k in range(S5_TILES):
        us_ref[k] = u[:, k * LANE:(k + 1) * LANE]
        for j in range(S5_L):
            u4_ref[k, :, j * LANE:(j + 1) * LANE] = us_ref[k, pl.ds(j, nrow, stride=S5_L), :].astype(BF16)
    tc = tc_ref[...]
    ts = ts_ref[...]
    cq = _rms(y[:, o_cq:o_ckv], qnw_ref[...]).astype(BF16)
    q = jnp.dot(cq, wuq_ref[...], preferred_element_type=F32) * ((MLA_NOPE + MLA_ROPE) ** -0.5 * LOG2E)
    for hh in range(MLA_HEADS):
        o = hh * MLA_QW
        qc_ref[:, o:o + MLA_NOPE] = q[:, o:o + MLA_NOPE].astype(BF16)
        qc_ref[:, o + MLA_NOPE:o + MLA_QW] = _rope(q[:, o + MLA_NOPE:o + MLA_QW], tc, ts).astype(BF16)
    ckv = _rms(y[:, o_ckv:o_kr], kvnw_ref[...]).astype(BF16)
    kv_ref[...] = jnp.dot(ckv, wukv_ref[...], preferred_element_type=F32).astype(BF16)
    kr_ref[...] = _rope(y[:, o_kr:o_kr + LANE], tc, ts).astype(BF16)


def _inproj(x, nw, winp, qnw, wuqp, kvnw, wukv, tabc, tabs, *, tm, seq):
    t, d = x.shape
    nseq = seq // tm
    row = lambda i: (i, 0)
    return pl.pallas_call(
        _inproj_kernel,
        grid=(t // tm,),
        in_specs=[
            pl.BlockSpec((tm, d), row),
            _const_spec((1, d)),
            _const_spec(winp.shape),
            _const_spec((1, MLA_Q_RANK)),
            _const_spec(wuqp.shape),
            _const_spec((1, MLA_KV_RANK)),
            _const_spec(wukv.shape),
            pl.BlockSpec((tm, LANE), lambda i: (i % nseq, 0)),
            pl.BlockSpec((tm, LANE), lambda i: (i % nseq, 0)),
        ],
        out_specs=[
            pl.BlockSpec((tm, 3 * SB_WIDTH), row),
            pl.BlockSpec((tm, SSM_WIDTH), row),
            pl.BlockSpec((S5_TILES, tm // S5_L, S5_W), lambda i: (0, i, 0)),
            pl.BlockSpec((tm, MLA_HEADS * MLA_QW), row),
            pl.BlockSpec((tm, MLA_HEADS * (MLA_NOPE + MLA_V)), row),
            pl.BlockSpec((tm, LANE), row),
        ],
        out_shape=[
            jax.ShapeDtypeStruct((t, 3 * SB_WIDTH), BF16),
            jax.ShapeDtypeStruct((t, SSM_WIDTH), F32),
            jax.ShapeDtypeStruct((S5_TILES, t // S5_L, S5_W), BF16),
            jax.ShapeDtypeStruct((t, MLA_HEADS * MLA_QW), BF16),
            jax.ShapeDtypeStruct((t, MLA_HEADS * (MLA_NOPE + MLA_V)), BF16),
            jax.ShapeDtypeStruct((t, LANE), BF16),
        ],
        scratch_shapes=[pltpu.VMEM((S5_TILES, tm, LANE), F32)],
        compiler_params=_cparams(("parallel",)),
        name="inproj",
    )(x, nw, winp, qnw, wuqp, kvnw, wukv, tabc, tabs)


def _rope_pad(w):
    z = jnp.zeros(w.shape[:-1] + (ROPE_HALF,), w.dtype)
    return jnp.concatenate([w[..., :ROPE_HALF], z, w[..., ROPE_HALF:], z], axis=-1)


def _prep_inproj(w_in, w_uq, w_ukv):
    o_kr = 3 * SB_WIDTH + SSM_WIDTH + MLA_Q_RANK + MLA_KV_RANK
    w_in = w_in.astype(BF16)
    winp = jnp.concatenate([w_in[:, :o_kr], _rope_pad(w_in[:, o_kr:])], axis=1)
    wq = w_uq.astype(BF16).reshape(MLA_Q_RANK, MLA_HEADS, MLA_NOPE + MLA_ROPE)
    wuqp = jnp.concatenate([wq[..., :MLA_NOPE], _rope_pad(wq[..., MLA_NOPE:])], axis=-1)
    wuqp = wuqp.reshape(MLA_Q_RANK, MLA_HEADS * MLA_QW)
    return winp, wuqp, w_ukv.astype(BF16)


def _rope_tables(seq):
    inv = ROPE_BASE ** (-jnp.arange(0, MLA_ROPE, 2, dtype=F32) / MLA_ROPE)
    ang = jnp.arange(seq, dtype=F32)[:, None] * inv[None, :]
    c, s = jnp.cos(ang), jnp.sin(ang)
    z = jnp.zeros_like(c)
    return jnp.concatenate([c, z, c, z], axis=1), jnp.concatenate([-s, z, s, z], axis=1)


def _sb_kernel(q_ref, k_ref, v_ref, u_ref, o_ref, z_s, lb_s, sp_s, cq_s, cb_s, w_s, acc_s, *, tq, tk):
    i = pl.program_id(2)
    nblk = (i + 1) * (tq // tk)
    assert tq // tk == 2
    first_head = lax.broadcasted_iota(jnp.int32, (1, LANE), 1) < SB_HEAD_DIM
    q = q_ref[...]
    zero = jnp.zeros_like(q)
    q2 = jnp.concatenate([jnp.where(first_head, q, zero), jnp.where(first_head, zero, q)], axis=0)
    umat = u_ref[...]
    rowpos = i * tq + lax.broadcasted_iota(jnp.int32, (2 * tq, 1), 0) % tq
    colofs = lax.broadcasted_iota(jnp.int32, (1, tk), 1)

    def start(t):
        return pl.multiple_of((nblk - 1 - t) * tk, tk)

    def s1(t):
        z_s[...] = lax.dot_general(q2, k_ref[pl.ds(start(t), tk), :], (((1,), (1,)), ((), ())),
                                   preferred_element_type=F32)

    def s2(t, masked):
        z = z_s[...]
        sp = jnp.maximum(z, 0.0) + jnp.log(1.0 + jnp.exp(-jnp.abs(z)))
        lb = z - sp
        if masked:
            strict = (start(t) + colofs) < rowpos
            sp = jnp.where(strict, sp, 0.0)
            lb = jnp.where(strict, lb, NEG_INF)
        lb_s[...] = lb
        sp_s[...] = sp.astype(BF16)
        carry = cb_s[...]
        cq_s[...] = carry
        cb_s[...] = carry + jnp.sum(sp, axis=-1, keepdims=True)

    def s3():
        tail = jnp.dot(sp_s[...], umat, preferred_element_type=F32)
        w_s[...] = jnp.exp(lb_s[...] - tail - cq_s[...]).astype(BF16)

    def s4(t):
        acc_s[...] += jnp.dot(w_s[...], v_ref[pl.ds(start(t), tk), :], preferred_element_type=F32)

    cb_s[...] = jnp.zeros_like(cb_s)
    acc_s[...] = jnp.zeros_like(acc_s)
    s1(0)
    s2(0, True)
    s1(1)
    s3()
    s2(1, True)

    @pl.when(i > 0)
    def _():
        s1(2)

        def body(t, c):
            s4(t - 3)
            s3()
            s2(t - 1, False)
            s1(t)
            return c

        lax.fori_loop(3, nblk, body, 0)
        s4(nblk - 3)
        s3()
        s2(nblk - 1, False)

    s4(nblk - 2)
    s3()
    s4(nblk - 1)
    acc = acc_s[...]
    o_ref[...] = jnp.where(first_head, acc[:tq], acc[tq:])


def _sb_attention(qkv, umat, *, batch, seq, tq, tk):
    t = qkv.shape[0]
    nq = seq // tq
    npair = SB_WIDTH // LANE
    return pl.pallas_call(
        functools.partial(_sb_kernel, tq=tq, tk=tk),
        grid=(batch, npair, nq),
        in_specs=[
            pl.BlockSpec((tq, LANE), lambda b, p, i: (b * nq + i, p)),
            pl.BlockSpec((seq, LANE), lambda b, p, i: (b, npair + p)),
            pl.BlockSpec((seq, LANE), lambda b, p, i: (b, 2 * npair + p)),
            pl.BlockSpec((tk, tk), lambda b, p, i: (0, 0)),
        ],
        out_specs=pl.BlockSpec((tq, LANE), lambda b, p, i: (b * nq + i, p)),
        out_shape=jax.ShapeDtypeStruct((t, SB_WIDTH), F32),
        scratch_shapes=[
            pltpu.VMEM((2 * tq, tk), F32),
            pltpu.VMEM((2 * tq, tk), F32),
            pltpu.VMEM((2 * tq, tk), BF16),
            pltpu.VMEM((2 * tq, 1), F32),
            pltpu.VMEM((2 * tq, 1), F32),
            pltpu.VMEM((2 * tq, tk), BF16),
            pltpu.VMEM((2 * tq, LANE), F32),
        ],
        compiler_params=_cparams(("parallel", "parallel", "arbitrary")),
        name="sb_attn",
    )(qkv, qkv, qkv, umat)


def _mla_kernel(q_ref, kn0_ref, v0_ref, kn1_ref, v1_ref, kr_ref, o_ref, s_s, m_s, l_s, acc_s, *, tq):
    i = pl.program_id(2)
    tk = tq
    q = q_ref[...]
    heads = ((q[:, :MLA_QW], kn0_ref, v0_ref), (q[:, MLA_QW:], kn1_ref, v1_ref))
    qchunk = (i * tq + lax.broadcasted_iota(jnp.int32, (tq, 1), 0)) // CHUNK
    colofs = lax.broadcasted_iota(jnp.int32, (1, tk), 1)

    def start(t):
        return pl.multiple_of((i - t) * tk, tk)

    def scores(t, masked):
        ks = start(t)
        krb = kr_ref[pl.ds(ks, tk), :]
        for hh, (qh, kn_ref, _) in enumerate(heads):
            kcat = jnp.concatenate([kn_ref[pl.ds(ks, tk), :], krb], axis=1)
            s = lax.dot_general(qh, kcat, (((1,), (1,)), ((), ())), preferred_element_type=F32)
            if masked:
                s = jnp.where((ks + colofs) // CHUNK <= qchunk, s, NEG_INF)
            s_s[hh] = s

    def update(t):
        for hh, (_, _, v_ref) in enumerate(heads):
            s = s_s[hh]
            m = m_s[hh]
            m_new = jnp.maximum(m, jnp.max(s, axis=-1, keepdims=True))
            alpha = jnp.exp2(m - m_new)
            p = jnp.exp2(s - m_new)
            m_s[hh] = m_new
            l_s[hh] = alpha * l_s[hh] + jnp.sum(p, axis=-1, keepdims=True)
            acc_s[hh] = alpha * acc_s[hh] + jnp.dot(p.astype(BF16), v_ref[pl.ds(start(t), tk), :],
                                                    preferred_element_type=F32)

    m_s[...] = jnp.full(m_s.shape, NEG_INF, F32)
    l_s[...] = jnp.zeros_like(l_s)
    acc_s[...] = jnp.zeros_like(acc_s)
    scores(0, True)

    def body(t, c):
        update(t - 1)
        scores(t, False)
        return c

    lax.fori_loop(1, i + 1, body, 0)
    update(i)
    o_ref[...] = jnp.concatenate([acc_s[0] / l_s[0], acc_s[1] / l_s[1]], axis=1)


def _mla_attention(qc, kv, kr, *, batch, seq, tq):
    t = qc.shape[0]
    nq = seq // tq
    kvspec = lambda col: pl.BlockSpec((seq, LANE), lambda b, p, i: (b, 4 * p + col))
    return pl.pallas_call(
        functools.partial(_mla_kernel, tq=tq),
        grid=(batch, MLA_HEADS // 2, nq),
        in_specs=[
            pl.BlockSpec((tq, 2 * MLA_QW), lambda b, p, i: (b * nq + i, p)),
            kvspec(0), kvspec(1), kvspec(2), kvspec(3),
            pl.BlockSpec((seq, LANE), lambda b, p, i: (b, 0)),
        ],
        out_specs=pl.BlockSpec((tq, 2 * MLA_V), lambda b, p, i: (b * nq + i, p)),
        out_shape=jax.ShapeDtypeStruct((t, MLA_WIDTH), F32),
        scratch_shapes=[
            pltpu.VMEM((2, tq, tq), F32),
            pltpu.VMEM((2, tq, 1), F32),
            pltpu.VMEM((2, tq, 1), F32),
            pltpu.VMEM((2, tq, MLA_V), F32),
        ],
        compiler_params=_cparams(("parallel", "parallel", "arbitrary")),
        name="mla_attn",
    )(qc, kv, kv, kv, kv, kr)


def _cmul(ar, ai, br, bi):
    return ar * br - ai * bi, ar * bi + ai * br


def _s5_prep_kernel(lrc_ref, lic_ref, dtc_ref, ctr_ref, cti_ref, lrr_ref, lir_ref, dtr_ref, btr_ref, bti_ref,
                    tmat_ref, bmat_ref, cmat_ref, pa_ref, pb_ref):
    col_g = lax.broadcasted_iota(jnp.int32, (1, LANE), 1) // SSM_GROUP
    row_g = lax.broadcasted_iota(jnp.int32, (LANE, 1), 0) // SSM_GROUP
    dtc = jnp.exp(dtc_ref[0])
    ar = lrc_ref[0] * dtc
    ai = lic_ref[0] * dtc
    lr, li = jnp.exp(ar) * jnp.cos(ai), jnp.exp(ar) * jnp.sin(ai)
    er, ei = jnp.ones_like(lr), jnp.zeros_like(li)
    vmats = []
    for d in range(S5_L + 1):
        cr, ci = _cmul(ctr_ref[0], cti_ref[0], er, ei)
        vmats.append(jnp.concatenate([cr, -ci], axis=0))
        er, ei = _cmul(er, ei, lr, li)
    zero_blk = jnp.zeros((LANE, LANE), BF16)
    for g in range(S5_GPT):
        for i in range(S5_L):
            cmat_ref[0, g * LANE:(g + 1) * LANE, i * LANE:(i + 1) * LANE] = (
                jnp.where(col_g == g, vmats[i + 1], 0.0).astype(BF16))
    dtr = jnp.exp(dtr_ref[0])
    lam_r = lrr_ref[0]
    lam_i = lir_ref[0]
    arr = lam_r * dtr
    air = lam_i * dtr
    lbr, lbi = jnp.exp(arr) * jnp.cos(air), jnp.exp(arr) * jnp.sin(air)
    nr, ni = lbr - 1.0, lbi
    den = lam_r * lam_r + lam_i * lam_i
    qr = (nr * lam_r + ni * lam_i) / den
    qi = (ni * lam_r - nr * lam_i) / den
    is_re = lax.broadcasted_iota(jnp.int32, (1, 2 * SSM_STATE), 1) < SSM_STATE
    fr, fi = jnp.ones_like(lbr), jnp.zeros_like(lbi)
    bbar_t = None
    for j in range(S5_L - 1, -1, -1):
        wr, wi = _cmul(qr, qi, fr, fi)
        br, bi = _cmul(wr, wi, btr_ref[0], bti_ref[0])
        bj = jnp.where(is_re, br, bi)
        if j == S5_L - 1:
            bbar_t = bj
        for g in range(S5_GPT):
            bmat_ref[0, j * LANE:(j + 1) * LANE, g * LANE:(g + 1) * LANE] = (
                jnp.where(row_g == g, bj, 0.0).astype(BF16))
        fr, fi = _cmul(fr, fi, lbr, lbi)
    cexp = jnp.concatenate(vmats[:S5_L], axis=1)
    dall = jnp.dot(bbar_t, cexp, preferred_element_type=F32, precision=lax.Precision.HIGHEST)
    for d in range(S5_L):
        blk = jnp.where(row_g == col_g, dall[:, d * LANE:(d + 1) * LANE], 0.0).astype(BF16)
        for j in range(S5_L - d):
            i = j + d
            tmat_ref[0, j * LANE:(j + 1) * LANE, i * LANE:(i + 1) * LANE] = blk
    for j in range(1, S5_L):
        for i in range(j):
            tmat_ref[0, j * LANE:(j + 1) * LANE, i * LANE:(i + 1) * LANE] = zero_blk
    zr, zi = fr, fi
    for k in range(pa_ref.shape[1]):
        pa_ref[0, k] = zr
        pb_ref[0, k] = jnp.where(is_re, -zi, zi)
        zr, zi = _cmul(zr, zi, zr, zi)


def _s5_prep(lam_re, lam_im, log_dt, b_re, b_im, c_re, c_im, *, steps):
    g, p = lam_re.shape
    hg = SSM_GROUP
    col = lambda a: jnp.repeat(a.reshape(S5_TILES, S5_GPT, p).transpose(0, 2, 1), hg, axis=2)
    row = lambda a: jnp.tile(jnp.repeat(a.reshape(S5_TILES, S5_GPT, p), hg, axis=1), (1, 1, 2))
    ctc = lambda c: c.reshape(S5_TILES, S5_GPT, hg, p).transpose(0, 3, 1, 2).reshape(S5_TILES, p, LANE)
    btr = lambda b: jnp.tile(b.reshape(S5_TILES, S5_GPT, p, hg).transpose(0, 1, 3, 2).reshape(S5_TILES, LANE, p),
                             (1, 1, 2))
    dt_col = jnp.repeat(log_dt.reshape(S5_TILES, 1, S5_GPT), hg, axis=2)
    dt_row = jnp.repeat(log_dt.reshape(S5_TILES, S5_GPT, 1), hg, axis=1)
    blk = lambda *s: pl.BlockSpec((1,) + s, lambda i: (i,) + (0,) * len(s))
    return pl.pallas_call(
        _s5_prep_kernel,
        grid=(S5_TILES,),
        in_specs=[blk(p, LANE), blk(p, LANE), blk(1, LANE), blk(p, LANE), blk(p, LANE),
                  blk(LANE, 2 * p), blk(LANE, 2 * p), blk(LANE, 1), blk(LANE, 2 * p), blk(LANE, 2 * p)],
        out_specs=[blk(S5_W, S5_W), blk(S5_W, S5_SW), blk(S5_SW, S5_W),
                   blk(steps, LANE, 2 * p), blk(steps, LANE, 2 * p)],
        out_shape=[
            jax.ShapeDtypeStruct((S5_TILES, S5_W, S5_W), BF16),
            jax.ShapeDtypeStruct((S5_TILES, S5_W, S5_SW), BF16),
            jax.ShapeDtypeStruct((S5_TILES, S5_SW, S5_W), BF16),
            jax.ShapeDtypeStruct((S5_TILES, steps, LANE, 2 * p), F32),
            jax.ShapeDtypeStruct((S5_TILES, steps, LANE, 2 * p), F32),
        ],
        compiler_params=_cparams(("parallel",)),
        name="s5_prep",
    )(col(lam_re), col(lam_im), dt_col, ctc(c_re), ctc(c_im),
      row(lam_re), row(lam_im), dt_row, btr(b_re), btr(b_im))


def _s5_scan_kernel(u_ref, tmat_ref, bmat_ref, cmat_ref, pa_ref, pb_ref, y_ref, *, nchunk):
    u = u_ref[0]
    rows = u.shape[0]
    x = jnp.dot(u, bmat_ref[0], preferred_element_type=F32)
    cidx = lax.broadcasted_iota(jnp.int32, (rows, 1), 0) % nchunk
    steps = pa_ref.shape[1]
    s_in = []
    for g in range(S5_GPT):
        xg = x[:, g * 2 * SSM_STATE:(g + 1) * 2 * SSM_STATE]
        r0 = g * SSM_GROUP
        for k in range(steps):
            sh = 1 << k
            xs = jnp.where(cidx >= sh, pltpu.roll(xg, sh, 0), 0.0)
            xg = (xg + pa_ref[0, k, r0:r0 + 1, :] * xs
                  + pb_ref[0, k, r0:r0 + 1, :] * pltpu.roll(xs, SSM_STATE, 1))
        s_in.append(jnp.where(cidx >= 1, pltpu.roll(xg, 1, 0), 0.0).astype(BF16))
    s_in = jnp.concatenate(s_in, axis=1)
    y_ref[0] = jnp.dot(s_in, cmat_ref[0], preferred_element_type=F32)
    nq = 4
    qw = S5_W // nq
    for c in range(nq):
        y_ref[0, :, c * qw:(c + 1) * qw] += jnp.dot(u[:, :(c + 1) * qw], tmat_ref[0, :(c + 1) * qw, c * qw:(c + 1) * qw],
                                                    preferred_element_type=F32)


def _s5_scan(u4, tmat, bmat, cmat, pa, pb, *, nchunk, rows):
    nt, total, w = u4.shape
    return pl.pallas_call(
        functools.partial(_s5_scan_kernel, nchunk=nchunk),
        grid=(nt, total // rows),
        in_specs=[
            pl.BlockSpec((1, rows, w), lambda k, r: (k, r, 0)),
            pl.BlockSpec((1,) + tmat.shape[1:], lambda k, r: (k, 0, 0)),
            pl.BlockSpec((1,) + bmat.shape[1:], lambda k, r: (k, 0, 0)),
            pl.BlockSpec((1,) + cmat.shape[1:], lambda k, r: (k, 0, 0)),
            pl.BlockSpec((1,) + pa.shape[1:], lambda k, r: (k, 0, 0, 0)),
            pl.BlockSpec((1,) + pb.shape[1:], lambda k, r: (k, 0, 0, 0)),
        ],
        out_specs=pl.BlockSpec((1, rows, w), lambda k, r: (k, r, 0)),
        out_shape=jax.ShapeDtypeStruct((nt, total, w), F32),
        compiler_params=_cparams(("parallel", "parallel")),
        name="s5_scan",
    )(u4, tmat, bmat, cmat, pa, pb)


def _outproj_kernel(x_ref, ysb_ref, y4_ref, u_ref, ymla_ref, d_ref, wglu_ref, mnw_ref, wout_ref, o_ref, ys_ref):
    mnw = mnw_ref[...]
    nrow = y4_ref.shape[1]
    for k in range(S5_TILES):
        for j in range(S5_L):
            ys_ref[k, pl.ds(j, nrow, stride=S5_L), :] = y4_ref[k, :, j * LANE:(j + 1) * LANE]
    yscan = jnp.concatenate([ys_ref[k] for k in range(S5_TILES)], axis=1)
    y = yscan + d_ref[...] * u_ref[...]
    g = jax.nn.gelu(y)
    gate = jax.nn.sigmoid(jnp.dot(g.astype(BF16), wglu_ref[...], preferred_element_type=F32))
    o_ssm = SB_WIDTH + SSM_WIDTH
    cat = jnp.concatenate([
        _rms(ysb_ref[...], mnw[:, :SB_WIDTH]).astype(BF16),
        _rms(g * gate, mnw[:, SB_WIDTH:o_ssm]).astype(BF16),
        _rms(ymla_ref[...], mnw[:, o_ssm:]).astype(BF16),
    ], axis=1)
    o_ref[...] = x_ref[...] + jnp.dot(cat, wout_ref[...], preferred_element_type=F32)


def _outproj(x, ysb, y4, u, ymla, d, wglu, mnw, wout, *, tm):
    t, dm = x.shape
    row = lambda i: (i, 0)
    return pl.pallas_call(
        _outproj_kernel,
        grid=(t // tm,),
        in_specs=[
            pl.BlockSpec((tm, dm), row),
            pl.BlockSpec((tm, SB_WIDTH), row),
            pl.BlockSpec((S5_TILES, tm // S5_L, S5_W), lambda i: (0, i, 0)),
            pl.BlockSpec((tm, SSM_WIDTH), row),
            pl.BlockSpec((tm, MLA_WIDTH), row),
            _const_spec((1, SSM_WIDTH)),
            _const_spec(wglu.shape),
            _const_spec((1, mnw.shape[1])),
            _const_spec(wout.shape),
        ],
        out_specs=pl.BlockSpec((tm, dm), row),
        out_shape=jax.ShapeDtypeStruct((t, dm), F32),
        scratch_shapes=[pltpu.VMEM((S5_TILES, tm, LANE), F32)],
        compiler_params=_cparams(("parallel",)),
        name="outproj",
    )(x, ysb, y4, u, ymla, d, wglu, mnw, wout)


FFN_TM = 1024
FFN_TF = 512
PROJ_TM = 512
SB_TQ = 512
SB_TK = 256
MLA_TQ = 512
S5_ROWS = 512


def kernel(x, norm_w, ffn_w13, ffn_w2, w_in, mla_q_norm_w, mla_w_uq, mla_kv_norm_w, mla_w_ukv,
           ssm_lam_re, ssm_lam_im, ssm_log_dt, ssm_b_re, ssm_b_im, ssm_c_re, ssm_c_im,
           ssm_d, ssm_w_glu, mix_norm_w, w_out, final_norm_w):
    batch, seq, dm = x.shape
    depth = norm_w.shape[0]
    t = batch * seq
    nchunk = seq // S5_L
    steps = nchunk.bit_length() - 1
    assert 1 << steps == nchunk
    s5_rows = min(S5_ROWS, batch * nchunk)
    assert s5_rows % nchunk == 0
    tabc, tabs = _rope_tables(seq)
    umat = (lax.broadcasted_iota(jnp.int32, (SB_TK, SB_TK), 0)
            > lax.broadcasted_iota(jnp.int32, (SB_TK, SB_TK), 1)).astype(BF16)
    fw = final_norm_w[None, :]
    xt = x.reshape(t, dm)
    for l in range(depth):
        xt = _ffn(xt, norm_w[l, 0][None], *_prep_ffn(ffn_w13[l, 0], ffn_w2[l, 0]), fw,
                  tm=FFN_TM, tf=FFN_TF, final=False)

        winp, wuqp, wukv = _prep_inproj(w_in[l], mla_w_uq[l], mla_w_ukv[l])
        qkv, u, u4, qc, kv, kr = _inproj(xt, norm_w[l, 1][None], winp, mla_q_norm_w[l][None], wuqp,
                                         mla_kv_norm_w[l][None], wukv, tabc, tabs, tm=PROJ_TM, seq=seq)
        ysb = _sb_attention(qkv, umat, batch=batch, seq=seq, tq=SB_TQ, tk=SB_TK)
        ymla = _mla_attention(qc, kv, kr, batch=batch, seq=seq, tq=MLA_TQ)

        tmat, bmat, cmat, pa, pb = _s5_prep(ssm_lam_re[l], ssm_lam_im[l], ssm_log_dt[l],
                                            ssm_b_re[l], ssm_b_im[l], ssm_c_re[l], ssm_c_im[l], steps=steps)
        y4 = _s5_scan(u4, tmat, bmat, cmat, pa, pb, nchunk=nchunk, rows=s5_rows)

        xt = _outproj(xt, ysb, y4, u, ymla, ssm_d[l][None], ssm_w_glu[l].astype(BF16),
                      mix_norm_w[l][None], w_out[l].astype(BF16), tm=PROJ_TM)

        xt = _ffn(xt, norm_w[l, 2][None], *_prep_ffn(ffn_w13[l, 1], ffn_w2[l, 1]), fw,
                  tm=FFN_TM, tf=FFN_TF, final=(l == depth - 1))
    return xt.reshape(batch, seq, dm)
```

```python
import functools
import math

import jax
import jax.numpy as jnp
from jax import lax
from jax.experimental import pallas as pl
from jax.experimental.pallas import tpu as pltpu

F32 = jnp.float32
BF16 = jnp.bfloat16

EPS = 1e-6
NEG_INF = -1e30
LOG2E = math.log2(math.e)
LANE = 128
VMEM_LIMIT = 56 * 1024 * 1024

CHUNK = 64
SB_HEADS = 8
SB_HEAD_DIM = 64
SB_WIDTH = SB_HEADS * SB_HEAD_DIM
SSM_WIDTH = 512
SSM_GROUP = 16
SSM_GROUPS = SSM_WIDTH // SSM_GROUP
SSM_STATE = 64
MLA_HEADS = 8
MLA_NOPE = 128
MLA_ROPE = 64
MLA_V = 128
MLA_Q_RANK = 512
MLA_KV_RANK = 256
MLA_WIDTH = MLA_HEADS * MLA_V
ROPE_BASE = 10000.0
ROPE_HALF = MLA_ROPE // 2
MLA_QW = MLA_NOPE + LANE

S5_L = 16
S5_GPT = LANE // SSM_GROUP
S5_TILES = SSM_WIDTH // LANE
S5_W = S5_L * LANE
S5_SW = S5_GPT * 2 * SSM_STATE


def _cparams(sem):
    return pltpu.CompilerParams(dimension_semantics=sem, vmem_limit_bytes=VMEM_LIMIT)


def _rms(x, w):
    ms = jnp.mean(x * x, axis=-1, keepdims=True)
    return x * lax.rsqrt(ms + EPS) * w


def _const_spec(shape):
    nd = len(shape)
    return pl.BlockSpec(shape, lambda *_: (0,) * nd, pipeline_mode=pl.Buffered(1))


def _ffn_kernel(x_ref, nw_ref, w1_ref, w3_ref, w2_ref, fw_ref, o_ref, h_ref, *, last, final):
    j = pl.program_id(1)
    nt = pl.num_programs(1)

    @pl.when(j == 0)
    def _():
        x = x_ref[...]
        h_ref[...] = _rms(x, nw_ref[...]).astype(BF16)
        o_ref[...] = x

    def tile(width):
        h = h_ref[...]
        a = jnp.dot(h, w1_ref[:, :width], preferred_element_type=F32)
        b = jnp.dot(h, w3_ref[:, :width], preferred_element_type=F32)
        g = (a * jax.nn.sigmoid(a)) * b * 0.5
        o_ref[...] += jnp.dot(g.astype(BF16), w2_ref[:width, :], preferred_element_type=F32)

    tf = w1_ref.shape[1]
    if last == tf:
        tile(tf)
    else:
        pl.when(j < nt - 1)(lambda: tile(tf))
        pl.when(j == nt - 1)(lambda: tile(last))

    if final:
        @pl.when(j == nt - 1)
        def _():
            o_ref[...] = _rms(o_ref[...], fw_ref[...])


def _ffn(x, nw, w1b, w3b, w2b, fw, *, layer, pos, tm, tf, final):
    t, d = x.shape
    ff = w2b.shape[2]
    nt = pl.cdiv(ff, tf)
    return pl.pallas_call(
        functools.partial(_ffn_kernel, last=ff - (nt - 1) * tf, final=final),
        grid=(t // tm, nt),
        in_specs=[
            pl.BlockSpec((tm, d), lambda i, j: (i, 0)),
            pl.BlockSpec((1, d), lambda i, j: (0, 0)),
            pl.BlockSpec((None, None, d, tf), lambda i, j: (layer, pos, 0, j)),
            pl.BlockSpec((None, None, d, tf), lambda i, j: (layer, pos, 0, j)),
            pl.BlockSpec((None, None, tf, d), lambda i, j: (layer, pos, j, 0)),
            pl.BlockSpec((1, d), lambda i, j: (0, 0)),
        ],
        out_specs=pl.BlockSpec((tm, d), lambda i, j: (i, 0)),
        out_shape=jax.ShapeDtypeStruct((t, d), F32),
        scratch_shapes=[pltpu.VMEM((tm, d), BF16)],
        compiler_params=_cparams(("parallel", "arbitrary")),
        name="ffn",
    )(x, nw, w1b, w3b, w2b, fw)


def _prep_ffn(ffn_w13, ffn_w2):
    ff = ffn_w13.shape[-1] // 2
    return ffn_w13[..., :ff].astype(BF16), ffn_w13[..., ff:].astype(BF16), ffn_w2.astype(BF16)


def _rope(r, tc, ts):
    return r * tc + pltpu.roll(r, 2 * ROPE_HALF, 1) * ts


def _inproj_kernel(x_ref, nw_ref, win_ref, qnw_ref, wuq_ref, kvnw_ref, wukv_ref, tc_ref, ts_ref,
                   qkv_ref, u_ref, u4_ref, qc_ref, kv_ref, kr_ref, us_ref):
    h = _rms(x_ref[...], nw_ref[...]).astype(BF16)
    y = jnp.dot(h, win_ref[...], preferred_element_type=F32)
    o_u = 3 * SB_WIDTH
    o_cq = o_u + SSM_WIDTH
    o_ckv = o_cq + MLA_Q_RANK
    o_kr = o_ckv + MLA_KV_RANK
    qkv_ref[:, :SB_WIDTH] = (y[:, :SB_WIDTH] * (SB_HEAD_DIM ** -0.5)).astype(BF16)
    qkv_ref[:, SB_WIDTH:] = y[:, SB_WIDTH:o_u].astype(BF16)
    u = y[:, o_u:o_cq]
    u_ref[...] = u
    nrow = u.shape[0] // S5_L
    for k in range(S5_TILES):
        us_ref[k] = u[:, k * LANE:(k + 1) * LANE]
        for j in range(S5_L):
            u4_ref[k, :, j * LANE:(j + 1) * LANE] = us_ref[k, pl.ds(j, nrow, stride=S5_L), :].astype(BF16)
    tc = tc_ref[...]
    ts = ts_ref[...]
    cq = _rms(y[:, o_cq:o_ckv], qnw_ref[...]).astype(BF16)
    q = jnp.dot(cq, wuq_ref[...], preferred_element_type=F32) * ((MLA_NOPE + MLA_ROPE) ** -0.5 * LOG2E)
    for hh in range(MLA_HEADS):
        o = hh * MLA_QW
        qc_ref[:, o:o + MLA_NOPE] = q[:, o:o + MLA_NOPE].astype(BF16)
        qc_ref[:, o + MLA_NOPE:o + MLA_QW] = _rope(q[:, o + MLA_NOPE:o + MLA_QW], tc, ts).astype(BF16)
    ckv = _rms(y[:, o_ckv:o_kr], kvnw_ref[...]).astype(BF16)
    kv_ref[...] = jnp.dot(ckv, wukv_ref[...], preferred_element_type=F32).astype(BF16)
    kr_ref[...] = _rope(y[:, o_kr:o_kr + LANE], tc, ts).astype(BF16)


def _inproj(x, nw, winp, qnw, wuqp, kvnw, wukv, tabc, tabs, *, tm, seq):
    t, d = x.shape
    nseq = seq // tm
    row = lambda i: (i, 0)
    return pl.pallas_call(
        _inproj_kernel,
        grid=(t // tm,),
        in_specs=[
            pl.BlockSpec((tm, d), row),
            _const_spec((1, d)),
            _const_spec(winp.shape),
            _const_spec((1, MLA_Q_RANK)),
            _const_spec(wuqp.shape),
            _const_spec((1, MLA_KV_RANK)),
            _const_spec(wukv.shape),
            pl.BlockSpec((tm, LANE), lambda i: (i % nseq, 0)),
            pl.BlockSpec((tm, LANE), lambda i: (i % nseq, 0)),
        ],
        out_specs=[
            pl.BlockSpec((tm, 3 * SB_WIDTH), row),
            pl.BlockSpec((tm, SSM_WIDTH), row),
            pl.BlockSpec((S5_TILES, tm // S5_L, S5_W), lambda i: (0, i, 0)),
            pl.BlockSpec((tm, MLA_HEADS * MLA_QW), row),
            pl.BlockSpec((tm, MLA_HEADS * (MLA_NOPE + MLA_V)), row),
            pl.BlockSpec((tm, LANE), row),
        ],
        out_shape=[
            jax.ShapeDtypeStruct((t, 3 * SB_WIDTH), BF16),
            jax.ShapeDtypeStruct((t, SSM_WIDTH), F32),
            jax.ShapeDtypeStruct((S5_TILES, t // S5_L, S5_W), BF16),
            jax.ShapeDtypeStruct((t, MLA_HEADS * MLA_QW), BF16),
            jax.ShapeDtypeStruct((t, MLA_HEADS * (MLA_NOPE + MLA_V)), BF16),
            jax.ShapeDtypeStruct((t, LANE), BF16),
        ],
        scratch_shapes=[pltpu.VMEM((S5_TILES, tm, LANE), F32)],
        compiler_params=_cparams(("parallel",)),
        name="inproj",
    )(x, nw, winp, qnw, wuqp, kvnw, wukv, tabc, tabs)


def _rope_pad(w):
    z = jnp.zeros(w.shape[:-1] + (ROPE_HALF,), w.dtype)
    return jnp.concatenate([w[..., :ROPE_HALF], z, w[..., ROPE_HALF:], z], axis=-1)


def _prep_inproj(w_in, w_uq, w_ukv):
    o_kr = 3 * SB_WIDTH + SSM_WIDTH + MLA_Q_RANK + MLA_KV_RANK
    w_in = w_in.astype(BF16)
    winp = jnp.concatenate([w_in[:, :o_kr], _rope_pad(w_in[:, o_kr:])], axis=1)
    wq = w_uq.astype(BF16).reshape(MLA_Q_RANK, MLA_HEADS, MLA_NOPE + MLA_ROPE)
    wuqp = jnp.concatenate([wq[..., :MLA_NOPE], _rope_pad(wq[..., MLA_NOPE:])], axis=-1)
    wuqp = wuqp.reshape(MLA_Q_RANK, MLA_HEADS * MLA_QW)
    return winp, wuqp, w_ukv.astype(BF16)


def _rope_tables(seq):
    inv = ROPE_BASE ** (-jnp.arange(0, MLA_ROPE, 2, dtype=F32) / MLA_ROPE)
    ang = jnp.arange(seq, dtype=F32)[:, None] * inv[None, :]
    c, s = jnp.cos(ang), jnp.sin(ang)
    z = jnp.zeros_like(c)
    return jnp.concatenate([c, z, c, z], axis=1), jnp.concatenate([-s, z, s, z], axis=1)


def _sb_kernel(q_ref, k_ref, v_ref, u_ref, o_ref, z_s, lb_s, sp_s, cq_s, cb_s, w_s, acc_s, *, tq, tk):
    i = pl.program_id(2)
    nblk = (i + 1) * (tq // tk)
    assert tq // tk == 2
    first_head = lax.broadcasted_iota(jnp.int32, (1, LANE), 1) < SB_HEAD_DIM
    q = q_ref[...]
    zero = jnp.zeros_like(q)
    q2 = jnp.concatenate([jnp.where(first_head, q, zero), jnp.where(first_head, zero, q)], axis=0)
    umat = u_ref[...]
    rowpos = i * tq + lax.broadcasted_iota(jnp.int32, (2 * tq, 1), 0) % tq
    colofs = lax.broadcasted_iota(jnp.int32, (1, tk), 1)

    def start(t):
        return pl.multiple_of((nblk - 1 - t) * tk, tk)

    def s1(t):
        z_s[...] = lax.dot_general(q2, k_ref[pl.ds(start(t), tk), :], (((1,), (1,)), ((), ())),
                                   preferred_element_type=F32)

    def s2(t, masked):
        z = z_s[...]
        sp = jnp.maximum(z, 0.0) + jnp.log(1.0 + jnp.exp(-jnp.abs(z)))
        lb = z - sp
        if masked:
            strict = (start(t) + colofs) < rowpos
            sp = jnp.where(strict, sp, 0.0)
            lb = jnp.where(strict, lb, NEG_INF)
        lb_s[...] = lb
        sp_s[...] = sp.astype(BF16)
        carry = cb_s[...]
        cq_s[...] = carry
        cb_s[...] = carry + jnp.sum(sp, axis=-1, keepdims=True)

    def s3():
        tail = jnp.dot(sp_s[...], umat, preferred_element_type=F32)
        w_s[...] = jnp.exp(lb_s[...] - tail - cq_s[...]).astype(BF16)

    def s4(t):
        acc_s[...] += jnp.dot(w_s[...], v_ref[pl.ds(start(t), tk), :], preferred_element_type=F32)

    cb_s[...] = jnp.zeros_like(cb_s)
    acc_s[...] = jnp.zeros_like(acc_s)
    s1(0)
    s2(0, True)
    s1(1)
    s3()
    s2(1, True)

    @pl.when(i > 0)
    def _():
        s1(2)

        def body(t, c):
            s4(t - 3)
            s3()
            s2(t - 1, False)
            s1(t)
            return c

        lax.fori_loop(3, nblk, body, 0)
        s4(nblk - 3)
        s3()
        s2(nblk - 1, False)

    s4(nblk - 2)
    s3()
    s4(nblk - 1)
    acc = acc_s[...]
    o_ref[...] = jnp.where(first_head, acc[:tq], acc[tq:])


def _sb_attention(qkv, umat, *, batch, seq, tq, tk):
    t = qkv.shape[0]
    nq = seq // tq
    npair = SB_WIDTH // LANE
    return pl.pallas_call(
        functools.partial(_sb_kernel, tq=tq, tk=tk),
        grid=(batch, npair, nq),
        in_specs=[
            pl.BlockSpec((tq, LANE), lambda b, p, i: (b * nq + i, p)),
            pl.BlockSpec((seq, LANE), lambda b, p, i: (b, npair + p)),
            pl.BlockSpec((seq, LANE), lambda b, p, i: (b, 2 * npair + p)),
            pl.BlockSpec((tk, tk), lambda b, p, i: (0, 0)),
        ],
        out_specs=pl.BlockSpec((tq, LANE), lambda b, p, i: (b * nq + i, p)),
        out_shape=jax.ShapeDtypeStruct((t, SB_WIDTH), F32),
        scratch_shapes=[
            pltpu.VMEM((2 * tq, tk), F32),
            pltpu.VMEM((2 * tq, tk), F32),
            pltpu.VMEM((2 * tq, tk), BF16),
            pltpu.VMEM((2 * tq, 1), F32),
            pltpu.VMEM((2 * tq, 1), F32),
            pltpu.VMEM((2 * tq, tk), BF16),
            pltpu.VMEM((2 * tq, LANE), F32),
        ],
        compiler_params=_cparams(("parallel", "parallel", "arbitrary")),
        name="sb_attn",
    )(qkv, qkv, qkv, umat)


def _mla_kernel(q_ref, *refs, tq, hps):
    kv_refs, (kr_ref, o_ref, s_s, m_s, l_s, acc_s) = refs[:2 * hps], refs[2 * hps:]
    i = pl.program_id(2)
    tk = tq
    q = q_ref[...]
    heads = tuple((q[:, hh * MLA_QW:(hh + 1) * MLA_QW], kv_refs[2 * hh], kv_refs[2 * hh + 1]) for hh in range(hps))
    qchunk = (i * tq + lax.broadcasted_iota(jnp.int32, (tq, 1), 0)) // CHUNK
    colofs = lax.broadcasted_iota(jnp.int32, (1, tk), 1)

    def start(t):
        return pl.multiple_of((i - t) * tk, tk)

    def scores(t, masked):
        ks = start(t)
        krb = kr_ref[pl.ds(ks, tk), :]
        for hh, (qh, kn_ref, _) in enumerate(heads):
            kcat = jnp.concatenate([kn_ref[pl.ds(ks, tk), :], krb], axis=1)
            s = lax.dot_general(qh, kcat, (((1,), (1,)), ((), ())), preferred_element_type=F32)
            if masked:
                s = jnp.where((ks + colofs) // CHUNK <= qchunk, s, NEG_INF)
            s_s[hh] = s

    def update(t):
        for hh, (_, _, v_ref) in enumerate(heads):
            s = s_s[hh]
            m = m_s[hh]
            m_new = jnp.maximum(m, jnp.max(s, axis=-1, keepdims=True))
            alpha = jnp.exp2(m - m_new)
            p = jnp.exp2(s - m_new)
            m_s[hh] = m_new
            l_s[hh] = alpha * l_s[hh] + jnp.sum(p, axis=-1, keepdims=True)
            acc_s[hh] = alpha * acc_s[hh] + jnp.dot(p.astype(BF16), v_ref[pl.ds(start(t), tk), :],
                                                    preferred_element_type=F32)

    m_s[...] = jnp.full(m_s.shape, NEG_INF, F32)
    l_s[...] = jnp.zeros_like(l_s)
    acc_s[...] = jnp.zeros_like(acc_s)
    scores(0, True)

    def body(t, c):
        update(t - 1)
        scores(t, False)
        return c

    lax.fori_loop(1, i + 1, body, 0)
    update(i)
    o_ref[...] = jnp.concatenate([acc_s[hh] / l_s[hh] for hh in range(hps)], axis=1)


def _mla_attention(qc, kv, kr, *, batch, seq, tq, hps):
    t = qc.shape[0]
    nq = seq // tq
    kvspec = lambda col: pl.BlockSpec((seq, LANE), lambda b, p, i: (b, 2 * hps * p + col))
    return pl.pallas_call(
        functools.partial(_mla_kernel, tq=tq, hps=hps),
        grid=(batch, MLA_HEADS // hps, nq),
        in_specs=[pl.BlockSpec((tq, hps * MLA_QW), lambda b, p, i: (b * nq + i, p))]
        + [kvspec(col) for col in range(2 * hps)]
        + [pl.BlockSpec((seq, LANE), lambda b, p, i: (b, 0))],
        out_specs=pl.BlockSpec((tq, hps * MLA_V), lambda b, p, i: (b * nq + i, p)),
        out_shape=jax.ShapeDtypeStruct((t, MLA_WIDTH), F32),
        scratch_shapes=[
            pltpu.VMEM((hps, tq, tq), F32),
            pltpu.VMEM((hps, tq, 1), F32),
            pltpu.VMEM((hps, tq, 1), F32),
            pltpu.VMEM((hps, tq, MLA_V), F32),
        ],
        compiler_params=_cparams(("parallel", "parallel", "arbitrary")),
        name="mla_attn",
    )(qc, *([kv] * (2 * hps)), kr)


def _cmul(ar, ai, br, bi):
    return ar * br - ai * bi, ar * bi + ai * br


def _s5_prep_kernel(lrc_ref, lic_ref, dtc_ref, ctr_ref, cti_ref, lrr_ref, lir_ref, dtr_ref, btr_ref, bti_ref,
                    tmat_ref, bmat_ref, cmat_ref, pa_ref, pb_ref):
    col_g = lax.broadcasted_iota(jnp.int32, (1, LANE), 1) // SSM_GROUP
    row_g = lax.broadcasted_iota(jnp.int32, (LANE, 1), 0) // SSM_GROUP
    dtc = jnp.exp(dtc_ref[0])
    ar = lrc_ref[0] * dtc
    ai = lic_ref[0] * dtc
    lr, li = jnp.exp(ar) * jnp.cos(ai), jnp.exp(ar) * jnp.sin(ai)
    er, ei = jnp.ones_like(lr), jnp.zeros_like(li)
    vmats = []
    for d in range(S5_L + 1):
        cr, ci = _cmul(ctr_ref[0], cti_ref[0], er, ei)
        vmats.append(jnp.concatenate([cr, -ci], axis=0))
        er, ei = _cmul(er, ei, lr, li)
    zero_blk = jnp.zeros((LANE, LANE), BF16)
    for g in range(S5_GPT):
        for i in range(S5_L):
            cmat_ref[0, g * LANE:(g + 1) * LANE, i * LANE:(i + 1) * LANE] = (
                jnp.where(col_g == g, vmats[i + 1], 0.0).astype(BF16))
    dtr = jnp.exp(dtr_ref[0])
    lam_r = lrr_ref[0]
    lam_i = lir_ref[0]
    arr = lam_r * dtr
    air = lam_i * dtr
    lbr, lbi = jnp.exp(arr) * jnp.cos(air), jnp.exp(arr) * jnp.sin(air)
    nr, ni = lbr - 1.0, lbi
    den = lam_r * lam_r + lam_i * lam_i
    qr = (nr * lam_r + ni * lam_i) / den
    qi = (ni * lam_r - nr * lam_i) / den
    is_re = lax.broadcasted_iota(jnp.int32, (1, 2 * SSM_STATE), 1) < SSM_STATE
    fr, fi = jnp.ones_like(lbr), jnp.zeros_like(lbi)
    bbar_t = None
    for j in range(S5_L - 1, -1, -1):
        wr, wi = _cmul(qr, qi, fr, fi)
        br, bi = _cmul(wr, wi, btr_ref[0], bti_ref[0])
        bj = jnp.where(is_re, br, bi)
        if j == S5_L - 1:
            bbar_t = bj
        for g in range(S5_GPT):
            bmat_ref[0, j * LANE:(j + 1) * LANE, g * LANE:(g + 1) * LANE] = (
                jnp.where(row_g == g, bj, 0.0).astype(BF16))
        fr, fi = _cmul(fr, fi, lbr, lbi)
    cexp = jnp.concatenate(vmats[:S5_L], axis=1)
    dall = jnp.dot(bbar_t, cexp, preferred_element_type=F32, precision=lax.Precision.HIGHEST)
    for d in range(S5_L):
        blk = jnp.where(row_g == col_g, dall[:, d * LANE:(d + 1) * LANE], 0.0).astype(BF16)
        for j in range(S5_L - d):
            i = j + d
            tmat_ref[0, j * LANE:(j + 1) * LANE, i * LANE:(i + 1) * LANE] = blk
    for j in range(1, S5_L):
        for i in range(j):
            tmat_ref[0, j * LANE:(j + 1) * LANE, i * LANE:(i + 1) * LANE] = zero_blk
    zr, zi = fr, fi
    for k in range(pa_ref.shape[1]):
        pa_ref[0, k] = zr
        pb_ref[0, k] = jnp.where(is_re, -zi, zi)
        zr, zi = _cmul(zr, zi, zr, zi)


def _s5_prep(lam_re, lam_im, log_dt, b_re, b_im, c_re, c_im, *, steps):
    g, p = lam_re.shape
    hg = SSM_GROUP
    col = lambda a: jnp.repeat(a.reshape(S5_TILES, S5_GPT, p).transpose(0, 2, 1), hg, axis=2)
    row = lambda a: jnp.tile(jnp.repeat(a.reshape(S5_TILES, S5_GPT, p), hg, axis=1), (1, 1, 2))
    ctc = lambda c: c.reshape(S5_TILES, S5_GPT, hg, p).transpose(0, 3, 1, 2).reshape(S5_TILES, p, LANE)
    btr = lambda b: jnp.tile(b.reshape(S5_TILES, S5_GPT, p, hg).transpose(0, 1, 3, 2).reshape(S5_TILES, LANE, p),
                             (1, 1, 2))
    dt_col = jnp.repeat(log_dt.reshape(S5_TILES, 1, S5_GPT), hg, axis=2)
    dt_row = jnp.repeat(log_dt.reshape(S5_TILES, S5_GPT, 1), hg, axis=1)
    blk = lambda *s: pl.BlockSpec((1,) + s, lambda i: (i,) + (0,) * len(s))
    return pl.pallas_call(
        _s5_prep_kernel,
        grid=(S5_TILES,),
        in_specs=[blk(p, LANE), blk(p, LANE), blk(1, LANE), blk(p, LANE), blk(p, LANE),
                  blk(LANE, 2 * p), blk(LANE, 2 * p), blk(LANE, 1), blk(LANE, 2 * p), blk(LANE, 2 * p)],
        out_specs=[blk(S5_W, S5_W), blk(S5_W, S5_SW), blk(S5_SW, S5_W),
                   blk(steps, LANE, 2 * p), blk(steps, LANE, 2 * p)],
        out_shape=[
            jax.ShapeDtypeStruct((S5_TILES, S5_W, S5_W), BF16),
            jax.ShapeDtypeStruct((S5_TILES, S5_W, S5_SW), BF16),
            jax.ShapeDtypeStruct((S5_TILES, S5_SW, S5_W), BF16),
            jax.ShapeDtypeStruct((S5_TILES, steps, LANE, 2 * p), F32),
            jax.ShapeDtypeStruct((S5_TILES, steps, LANE, 2 * p), F32),
        ],
        compiler_params=_cparams(("parallel",)),
        name="s5_prep",
    )(col(lam_re), col(lam_im), dt_col, ctc(c_re), ctc(c_im),
      row(lam_re), row(lam_im), dt_row, btr(b_re), btr(b_im))


def _s5_scan_kernel(u_ref, tmat_ref, bmat_ref, cmat_ref, pa_ref, pb_ref, y_ref, *, nchunk):
    u = u_ref[0]
    rows = u.shape[0]
    x = jnp.dot(u, bmat_ref[0], preferred_element_type=F32)
    cidx = lax.broadcasted_iota(jnp.int32, (rows, 1), 0) % nchunk
    steps = pa_ref.shape[1]
    s_in = []
    for g in range(S5_GPT):
        xg = x[:, g * 2 * SSM_STATE:(g + 1) * 2 * SSM_STATE]
        r0 = g * SSM_GROUP
        for k in range(steps):
            sh = 1 << k
            xs = jnp.where(cidx >= sh, pltpu.roll(xg, sh, 0), 0.0)
            xg = (xg + pa_ref[0, k, r0:r0 + 1, :] * xs
                  + pb_ref[0, k, r0:r0 + 1, :] * pltpu.roll(xs, SSM_STATE, 1))
        s_in.append(jnp.where(cidx >= 1, pltpu.roll(xg, 1, 0), 0.0).astype(BF16))
    s_in = jnp.concatenate(s_in, axis=1)
    y_ref[0] = jnp.dot(s_in, cmat_ref[0], preferred_element_type=F32)
    nq = 4
    qw = S5_W // nq
    for c in range(nq):
        y_ref[0, :, c * qw:(c + 1) * qw] += jnp.dot(u[:, :(c + 1) * qw], tmat_ref[0, :(c + 1) * qw, c * qw:(c + 1) * qw],
                                                    preferred_element_type=F32)


def _s5_scan(u4, tmat, bmat, cmat, pa, pb, *, nchunk, rows):
    nt, total, w = u4.shape
    return pl.pallas_call(
        functools.partial(_s5_scan_kernel, nchunk=nchunk),
        grid=(nt, total // rows),
        in_specs=[
            pl.BlockSpec((1, rows, w), lambda k, r: (k, r, 0)),
            pl.BlockSpec((1,) + tmat.shape[1:], lambda k, r: (k, 0, 0)),
            pl.BlockSpec((1,) + bmat.shape[1:], lambda k, r: (k, 0, 0)),
            pl.BlockSpec((1,) + cmat.shape[1:], lambda k, r: (k, 0, 0)),
            pl.BlockSpec((1,) + pa.shape[1:], lambda k, r: (k, 0, 0, 0)),
            pl.BlockSpec((1,) + pb.shape[1:], lambda k, r: (k, 0, 0, 0)),
        ],
        out_specs=pl.BlockSpec((1, rows, w), lambda k, r: (k, r, 0)),
        out_shape=jax.ShapeDtypeStruct((nt, total, w), F32),
        compiler_params=_cparams(("parallel", "parallel")),
        name="s5_scan",
    )(u4, tmat, bmat, cmat, pa, pb)


def _outproj_kernel(x_ref, ysb_ref, y4_ref, u_ref, ymla_ref, d_ref, wglu_ref, mnw_ref, wout_ref, o_ref, ys_ref):
    mnw = mnw_ref[...]
    nrow = y4_ref.shape[1]
    for k in range(S5_TILES):
        for j in range(S5_L):
            ys_ref[k, pl.ds(j, nrow, stride=S5_L), :] = y4_ref[k, :, j * LANE:(j + 1) * LANE]
    yscan = jnp.concatenate([ys_ref[k] for k in range(S5_TILES)], axis=1)
    y = yscan + d_ref[...] * u_ref[...]
    g = jax.nn.gelu(y)
    gate = jax.nn.sigmoid(jnp.dot(g.astype(BF16), wglu_ref[...], preferred_element_type=F32))
    o_ssm = SB_WIDTH + SSM_WIDTH
    cat = jnp.concatenate([
        _rms(ysb_ref[...], mnw[:, :SB_WIDTH]).astype(BF16),
        _rms(g * gate, mnw[:, SB_WIDTH:o_ssm]).astype(BF16),
        _rms(ymla_ref[...], mnw[:, o_ssm:]).astype(BF16),
    ], axis=1)
    o_ref[...] = x_ref[...] + jnp.dot(cat, wout_ref[...], preferred_element_type=F32)


def _outproj(x, ysb, y4, u, ymla, d, wglu, mnw, wout, *, tm):
    t, dm = x.shape
    row = lambda i: (i, 0)
    return pl.pallas_call(
        _outproj_kernel,
        grid=(t // tm,),
        in_specs=[
            pl.BlockSpec((tm, dm), row),
            pl.BlockSpec((tm, SB_WIDTH), row),
            pl.BlockSpec((S5_TILES, tm // S5_L, S5_W), lambda i: (0, i, 0)),
            pl.BlockSpec((tm, SSM_WIDTH), row),
            pl.BlockSpec((tm, MLA_WIDTH), row),
            _const_spec((1, SSM_WIDTH)),
            _const_spec(wglu.shape),
            _const_spec((1, mnw.shape[1])),
            _const_spec(wout.shape),
        ],
        out_specs=pl.BlockSpec((tm, dm), row),
        out_shape=jax.ShapeDtypeStruct((t, dm), F32),
        scratch_shapes=[pltpu.VMEM((S5_TILES, tm, LANE), F32)],
        compiler_params=_cparams(("parallel",)),
        name="outproj",
    )(x, ysb, y4, u, ymla, d, wglu, mnw, wout)


FFN_TM = 1024
FFN_TF = 512
PROJ_TM = 512
SB_TQ = 512
SB_TK = 256
MLA_TQ = 512
MLA_HPS = 4
S5_ROWS = 512


def kernel(x, norm_w, ffn_w13, ffn_w2, w_in, mla_q_norm_w, mla_w_uq, mla_kv_norm_w, mla_w_ukv,
           ssm_lam_re, ssm_lam_im, ssm_log_dt, ssm_b_re, ssm_b_im, ssm_c_re, ssm_c_im,
           ssm_d, ssm_w_glu, mix_norm_w, w_out, final_norm_w):
    batch, seq, dm = x.shape
    depth = norm_w.shape[0]
    t = batch * seq
    nchunk = seq // S5_L
    steps = nchunk.bit_length() - 1
    assert 1 << steps == nchunk
    s5_rows = min(S5_ROWS, batch * nchunk)
    assert s5_rows % nchunk == 0
    tabc, tabs = _rope_tables(seq)
    umat = (lax.broadcasted_iota(jnp.int32, (SB_TK, SB_TK), 0)
            > lax.broadcasted_iota(jnp.int32, (SB_TK, SB_TK), 1)).astype(BF16)
    fw = final_norm_w[None, :]
    w1b, w3b, w2b = _prep_ffn(ffn_w13, ffn_w2)
    xt = x.reshape(t, dm)
    for l in range(depth):
        xt = _ffn(xt, norm_w[l, 0][None], w1b, w3b, w2b, fw, layer=l, pos=0, tm=FFN_TM, tf=FFN_TF, final=False)

        winp, wuqp, wukv = _prep_inproj(w_in[l], mla_w_uq[l], mla_w_ukv[l])
        qkv, u, u4, qc, kv, kr = _inproj(xt, norm_w[l, 1][None], winp, mla_q_norm_w[l][None], wuqp,
                                         mla_kv_norm_w[l][None], wukv, tabc, tabs, tm=PROJ_TM, seq=seq)
        ysb = _sb_attention(qkv, umat, batch=batch, seq=seq, tq=SB_TQ, tk=SB_TK)
        ymla = _mla_attention(qc, kv, kr, batch=batch, seq=seq, tq=MLA_TQ, hps=MLA_HPS)

        tmat, bmat, cmat, pa, pb = _s5_prep(ssm_lam_re[l], ssm_lam_im[l], ssm_log_dt[l],
                                            ssm_b_re[l], ssm_b_im[l], ssm_c_re[l], ssm_c_im[l], steps=steps)
        y4 = _s5_scan(u4, tmat, bmat, cmat, pa, pb, nchunk=nchunk, rows=s5_rows)

        xt = _outproj(xt, ysb, y4, u, ymla, ssm_d[l][None], ssm_w_glu[l].astype(BF16),
                      mix_norm_w[l][None], w_out[l].astype(BF16), tm=PROJ_TM)

        xt = _ffn(xt, norm_w[l, 2][None], w1b, w3b, w2b, fw, layer=l, pos=1, tm=FFN_TM, tf=FFN_TF,
                  final=(l == depth - 1))
    return xt.reshape(batch, seq, dm)
```

```python
import functools
import math

import jax
import jax.numpy as jnp
from jax import lax
from jax.experimental import pallas as pl
from jax.experimental.pallas import tpu as pltpu

F32 = jnp.float32
BF16 = jnp.bfloat16

EPS = 1e-6
NEG_INF = -1e30
LOG2E = math.log2(math.e)
LANE = 128
VMEM_LIMIT = 56 * 1024 * 1024

CHUNK = 64
SB_HEADS = 8
SB_HEAD_DIM = 64
SB_WIDTH = SB_HEADS * SB_HEAD_DIM
SSM_WIDTH = 512
SSM_GROUP = 16
SSM_GROUPS = SSM_WIDTH // SSM_GROUP
SSM_STATE = 64
MLA_HEADS = 8
MLA_NOPE = 128
MLA_ROPE = 64
MLA_V = 128
MLA_Q_RANK = 512
MLA_KV_RANK = 256
MLA_WIDTH = MLA_HEADS * MLA_V
ROPE_BASE = 10000.0
ROPE_HALF = MLA_ROPE // 2
MLA_QW = MLA_NOPE + LANE

S5_L = 16
S5_GPT = LANE // SSM_GROUP
S5_TILES = SSM_WIDTH // LANE
S5_W = S5_L * LANE
S5_SW = S5_GPT * 2 * SSM_STATE


def _cparams(sem):
    return pltpu.CompilerParams(dimension_semantics=sem, vmem_limit_bytes=VMEM_LIMIT)


def _rms(x, w):
    ms = jnp.mean(x * x, axis=-1, keepdims=True)
    return x * lax.rsqrt(ms + EPS) * w


def _const_spec(shape):
    nd = len(shape)
    return pl.BlockSpec(shape, lambda *_: (0,) * nd, pipeline_mode=pl.Buffered(1))


def _ffn_kernel(x_ref, nw_ref, w1_ref, w3_ref, w2_ref, fw_ref, o_ref, h_ref, *, last, final):
    j = pl.program_id(1)
    nt = pl.num_programs(1)

    @pl.when(j == 0)
    def _():
        x = x_ref[...]
        h_ref[...] = _rms(x, nw_ref[...]).astype(BF16)
        o_ref[...] = x

    def tile(width):
        h = h_ref[...]
        a = jnp.dot(h, w1_ref[:, :width], preferred_element_type=F32)
        b = jnp.dot(h, w3_ref[:, :width], preferred_element_type=F32)
        g = (a * jax.nn.sigmoid(a)) * b * 0.5
        o_ref[...] += jnp.dot(g.astype(BF16), w2_ref[:width, :], preferred_element_type=F32)

    tf = w1_ref.shape[1]
    if last == tf:
        tile(tf)
    else:
        pl.when(j < nt - 1)(lambda: tile(tf))
        pl.when(j == nt - 1)(lambda: tile(last))

    if final:
        @pl.when(j == nt - 1)
        def _():
            o_ref[...] = _rms(o_ref[...], fw_ref[...])


def _ffn(x, nw, w13h, w2b, fw, *, layer, pos, tm, tf, final):
    t, d = x.shape
    ff = w2b.shape[2]
    nt = pl.cdiv(ff, tf)
    return pl.pallas_call(
        functools.partial(_ffn_kernel, last=ff - (nt - 1) * tf, final=final),
        grid=(t // tm, nt),
        in_specs=[
            pl.BlockSpec((tm, d), lambda i, j: (i, 0)),
            pl.BlockSpec((1, d), lambda i, j: (0, 0)),
            pl.BlockSpec((None, None, None, d, tf), lambda i, j: (layer, pos, 0, 0, j)),
            pl.BlockSpec((None, None, None, d, tf), lambda i, j: (layer, pos, 1, 0, j)),
            pl.BlockSpec((None, None, tf, d), lambda i, j: (layer, pos, j, 0)),
            pl.BlockSpec((1, d), lambda i, j: (0, 0)),
        ],
        out_specs=pl.BlockSpec((tm, d), lambda i, j: (i, 0)),
        out_shape=jax.ShapeDtypeStruct((t, d), F32),
        scratch_shapes=[pltpu.VMEM((tm, d), BF16)],
        compiler_params=_cparams(("parallel", "arbitrary")),
        name="ffn",
    )(x, nw, w13h, w13h, w2b, fw)


def _cast_kernel(w_ref, o_ref):
    o_ref[...] = w_ref[...].astype(o_ref.dtype)


def _prep_ffn(ffn_w13, ffn_w2, *, rows):
    depth, npos, d, two_ff = ffn_w13.shape
    ff = two_ff // 2
    w13h = pl.pallas_call(
        _cast_kernel,
        grid=(depth, npos, 2, d // rows),
        in_specs=[pl.BlockSpec((None, None, rows, ff), lambda l, k, h, r: (l, k, r, h))],
        out_specs=pl.BlockSpec((None, None, None, rows, ff), lambda l, k, h, r: (l, k, h, r, 0)),
        out_shape=jax.ShapeDtypeStruct((depth, npos, 2, d, ff), BF16),
        compiler_params=_cparams(("parallel",) * 4),
        name="w13_cast",
    )(ffn_w13)
    return w13h, ffn_w2.astype(BF16)


def _rope(r, tc, ts):
    return r * tc + pltpu.roll(r, 2 * ROPE_HALF, 1) * ts


def _inproj_kernel(x_ref, nw_ref, win_ref, qnw_ref, wuq_ref, kvnw_ref, wukv_ref, tc_ref, ts_ref,
                   qkv_ref, u_ref, u4_ref, qc_ref, kv_ref, kr_ref, us_ref):
    h = _rms(x_ref[...], nw_ref[...]).astype(BF16)
    y = jnp.dot(h, win_ref[...], preferred_element_type=F32)
    o_u = 3 * SB_WIDTH
    o_cq = o_u + SSM_WIDTH
    o_ckv = o_cq + MLA_Q_RANK
    o_kr = o_ckv + MLA_KV_RANK
    qkv_ref[:, :SB_WIDTH] = (y[:, :SB_WIDTH] * (SB_HEAD_DIM ** -0.5)).astype(BF16)
    qkv_ref[:, SB_WIDTH:] = y[:, SB_WIDTH:o_u].astype(BF16)
    u = y[:, o_u:o_cq]
    u_ref[...] = u
    nrow = u.shape[0] // S5_L
    for k in range(S5_TILES):
        us_ref[k] = u[:, k * LANE:(k + 1) * LANE]
        for j in range(S5_L):
            u4_ref[k, :, j * LANE:(j + 1) * LANE] = us_ref[k, pl.ds(j, nrow, stride=S5_L), :].astype(BF16)
    tc = tc_ref[...]
    ts = ts_ref[...]
    cq = _rms(y[:, o_cq:o_ckv], qnw_ref[...]).astype(BF16)
    q = jnp.dot(cq, wuq_ref[...], preferred_element_type=F32) * ((MLA_NOPE + MLA_ROPE) ** -0.5 * LOG2E)
    for hh in range(MLA_HEADS):
        o = hh * MLA_QW
        qc_ref[:, o:o + MLA_NOPE] = q[:, o:o + MLA_NOPE].astype(BF16)
        qc_ref[:, o + MLA_NOPE:o + MLA_QW] = _rope(q[:, o + MLA_NOPE:o + MLA_QW], tc, ts).astype(BF16)
    ckv = _rms(y[:, o_ckv:o_kr], kvnw_ref[...]).astype(BF16)
    kv_ref[...] = jnp.dot(ckv, wukv_ref[...], preferred_element_type=F32).astype(BF16)
    kr_ref[...] = _rope(y[:, o_kr:o_kr + LANE], tc, ts).astype(BF16)


def _inproj(x, nw, winp, qnw, wuqp, kvnw, wukv, tabc, tabs, *, tm, seq):
    t, d = x.shape
    nseq = seq // tm
    row = lambda i: (i, 0)
    return pl.pallas_call(
        _inproj_kernel,
        grid=(t // tm,),
        in_specs=[
            pl.BlockSpec((tm, d), row),
            _const_spec((1, d)),
            _const_spec(winp.shape),
            _const_spec((1, MLA_Q_RANK)),
            _const_spec(wuqp.shape),
            _const_spec((1, MLA_KV_RANK)),
            _const_spec(wukv.shape),
            pl.BlockSpec((tm, LANE), lambda i: (i % nseq, 0)),
            pl.BlockSpec((tm, LANE), lambda i: (i % nseq, 0)),
        ],
        out_specs=[
            pl.BlockSpec((tm, 3 * SB_WIDTH), row),
            pl.BlockSpec((tm, SSM_WIDTH), row),
            pl.BlockSpec((S5_TILES, tm // S5_L, S5_W), lambda i: (0, i, 0)),
            pl.BlockSpec((tm, MLA_HEADS * MLA_QW), row),
            pl.BlockSpec((tm, MLA_HEADS * (MLA_NOPE + MLA_V)), row),
            pl.BlockSpec((tm, LANE), row),
        ],
        out_shape=[
            jax.ShapeDtypeStruct((t, 3 * SB_WIDTH), BF16),
            jax.ShapeDtypeStruct((t, SSM_WIDTH), F32),
            jax.ShapeDtypeStruct((S5_TILES, t // S5_L, S5_W), BF16),
            jax.ShapeDtypeStruct((t, MLA_HEADS * MLA_QW), BF16),
            jax.ShapeDtypeStruct((t, MLA_HEADS * (MLA_NOPE + MLA_V)), BF16),
            jax.ShapeDtypeStruct((t, LANE), BF16),
        ],
        scratch_shapes=[pltpu.VMEM((S5_TILES, tm, LANE), F32)],
        compiler_params=_cparams(("parallel",)),
        name="inproj",
    )(x, nw, winp, qnw, wuqp, kvnw, wukv, tabc, tabs)


def _rope_pad(w):
    z = jnp.zeros(w.shape[:-1] + (ROPE_HALF,), w.dtype)
    return jnp.concatenate([w[..., :ROPE_HALF], z, w[..., ROPE_HALF:], z], axis=-1)


def _prep_inproj(w_in, w_uq, w_ukv):
    o_kr = 3 * SB_WIDTH + SSM_WIDTH + MLA_Q_RANK + MLA_KV_RANK
    w_in = w_in.astype(BF16)
    winp = jnp.concatenate([w_in[:, :o_kr], _rope_pad(w_in[:, o_kr:])], axis=1)
    wq = w_uq.astype(BF16).reshape(MLA_Q_RANK, MLA_HEADS, MLA_NOPE + MLA_ROPE)
    wuqp = jnp.concatenate([wq[..., :MLA_NOPE], _rope_pad(wq[..., MLA_NOPE:])], axis=-1)
    wuqp = wuqp.reshape(MLA_Q_RANK, MLA_HEADS * MLA_QW)
    return winp, wuqp, w_ukv.astype(BF16)


def _rope_tables(seq):
    inv = ROPE_BASE ** (-jnp.arange(0, MLA_ROPE, 2, dtype=F32) / MLA_ROPE)
    ang = jnp.arange(seq, dtype=F32)[:, None] * inv[None, :]
    c, s = jnp.cos(ang), jnp.sin(ang)
    z = jnp.zeros_like(c)
    return jnp.concatenate([c, z, c, z], axis=1), jnp.concatenate([-s, z, s, z], axis=1)


def _sb_kernel(q_ref, k_ref, v_ref, u_ref, o_ref, z_s, lb_s, sp_s, cq_s, cb_s, w_s, acc_s, *, tq, tk):
    i = pl.program_id(2)
    nblk = (i + 1) * (tq // tk)
    assert tq // tk == 2
    first_head = lax.broadcasted_iota(jnp.int32, (1, LANE), 1) < SB_HEAD_DIM
    q = q_ref[...]
    zero = jnp.zeros_like(q)
    q2 = jnp.concatenate([jnp.where(first_head, q, zero), jnp.where(first_head, zero, q)], axis=0)
    umat = u_ref[...]
    rowpos = i * tq + lax.broadcasted_iota(jnp.int32, (2 * tq, 1), 0) % tq
    colofs = lax.broadcasted_iota(jnp.int32, (1, tk), 1)

    def start(t):
        return pl.multiple_of((nblk - 1 - t) * tk, tk)

    def s1(t):
        z_s[...] = lax.dot_general(q2, k_ref[pl.ds(start(t), tk), :], (((1,), (1,)), ((), ())),
                                   preferred_element_type=F32)

    def s2(t, masked):
        z = z_s[...]
        sp = jnp.maximum(z, 0.0) + jnp.log(1.0 + jnp.exp(-jnp.abs(z)))
        lb = z - sp
        if masked:
            strict = (start(t) + colofs) < rowpos
            sp = jnp.where(strict, sp, 0.0)
            lb = jnp.where(strict, lb, NEG_INF)
        lb_s[...] = lb
        sp_s[...] = sp.astype(BF16)
        carry = cb_s[...]
        cq_s[...] = carry
        cb_s[...] = carry + jnp.sum(sp, axis=-1, keepdims=True)

    def s3():
        tail = jnp.dot(sp_s[...], umat, preferred_element_type=F32)
        w_s[...] = jnp.exp(lb_s[...] - tail - cq_s[...]).astype(BF16)

    def s4(t):
        acc_s[...] += jnp.dot(w_s[...], v_ref[pl.ds(start(t), tk), :], preferred_element_type=F32)

    cb_s[...] = jnp.zeros_like(cb_s)
    acc_s[...] = jnp.zeros_like(acc_s)
    s1(0)
    s2(0, True)
    s1(1)
    s3()
    s2(1, True)

    @pl.when(i > 0)
    def _():
        s1(2)

        def body(t, c):
            s4(t - 3)
            s3()
            s2(t - 1, False)
            s1(t)
            return c

        lax.fori_loop(3, nblk, body, 0)
        s4(nblk - 3)
        s3()
        s2(nblk - 1, False)

    s4(nblk - 2)
    s3()
    s4(nblk - 1)
    acc = acc_s[...]
    o_ref[...] = jnp.where(first_head, acc[:tq], acc[tq:])


def _sb_attention(qkv, umat, *, batch, seq, tq, tk):
    t = qkv.shape[0]
    nq = seq // tq
    npair = SB_WIDTH // LANE
    return pl.pallas_call(
        functools.partial(_sb_kernel, tq=tq, tk=tk),
        grid=(batch, npair, nq),
        in_specs=[
            pl.BlockSpec((tq, LANE), lambda b, p, i: (b * nq + i, p)),
            pl.BlockSpec((seq, LANE), lambda b, p, i: (b, npair + p)),
            pl.BlockSpec((seq, LANE), lambda b, p, i: (b, 2 * npair + p)),
            pl.BlockSpec((tk, tk), lambda b, p, i: (0, 0)),
        ],
        out_specs=pl.BlockSpec((tq, LANE), lambda b, p, i: (b * nq + i, p)),
        out_shape=jax.ShapeDtypeStruct((t, SB_WIDTH), F32),
        scratch_shapes=[
            pltpu.VMEM((2 * tq, tk), F32),
            pltpu.VMEM((2 * tq, tk), F32),
            pltpu.VMEM((2 * tq, tk), BF16),
            pltpu.VMEM((2 * tq, 1), F32),
            pltpu.VMEM((2 * tq, 1), F32),
            pltpu.VMEM((2 * tq, tk), BF16),
            pltpu.VMEM((2 * tq, LANE), F32),
        ],
        compiler_params=_cparams(("parallel", "parallel", "arbitrary")),
        name="sb_attn",
    )(qkv, qkv, qkv, umat)


def _mla_kernel(q_ref, *refs, tq, hps):
    kv_refs, (kr_ref, o_ref, s_s, m_s, l_s, acc_s) = refs[:2 * hps], refs[2 * hps:]
    i = pl.program_id(2)
    tk = tq
    q = q_ref[...]
    heads = tuple((q[:, hh * MLA_QW:(hh + 1) * MLA_QW], kv_refs[2 * hh], kv_refs[2 * hh + 1]) for hh in range(hps))
    qchunk = (i * tq + lax.broadcasted_iota(jnp.int32, (tq, 1), 0)) // CHUNK
    colofs = lax.broadcasted_iota(jnp.int32, (1, tk), 1)

    def start(t):
        return pl.multiple_of((i - t) * tk, tk)

    def scores(t, masked):
        ks = start(t)
        krb = kr_ref[pl.ds(ks, tk), :]
        for hh, (qh, kn_ref, _) in enumerate(heads):
            kcat = jnp.concatenate([kn_ref[pl.ds(ks, tk), :], krb], axis=1)
            s = lax.dot_general(qh, kcat, (((1,), (1,)), ((), ())), preferred_element_type=F32)
            if masked:
                s = jnp.where((ks + colofs) // CHUNK <= qchunk, s, NEG_INF)
            s_s[hh] = s

    def update(t):
        for hh, (_, _, v_ref) in enumerate(heads):
            s = s_s[hh]
            m = m_s[hh]
            m_new = jnp.maximum(m, jnp.max(s, axis=-1, keepdims=True))
            alpha = jnp.exp2(m - m_new)
            p = jnp.exp2(s - m_new)
            m_s[hh] = m_new
            l_s[hh] = alpha * l_s[hh] + jnp.sum(p, axis=-1, keepdims=True)
            acc_s[hh] = alpha * acc_s[hh] + jnp.dot(p.astype(BF16), v_ref[pl.ds(start(t), tk), :],
                                                    preferred_element_type=F32)

    m_s[...] = jnp.full(m_s.shape, NEG_INF, F32)
    l_s[...] = jnp.zeros_like(l_s)
    acc_s[...] = jnp.zeros_like(acc_s)
    scores(0, True)

    def body(t, c):
        update(t - 1)
        scores(t, False)
        return c

    lax.fori_loop(1, i + 1, body, 0)
    update(i)
    o_ref[...] = jnp.concatenate([acc_s[hh] / l_s[hh] for hh in range(hps)], axis=1)


def _mla_attention(qc, kv, kr, *, batch, seq, tq, hps):
    t = qc.shape[0]
    nq = seq // tq
    kvspec = lambda col: pl.BlockSpec((seq, LANE), lambda b, p, i: (b, 2 * hps * p + col))
    return pl.pallas_call(
        functools.partial(_mla_kernel, tq=tq, hps=hps),
        grid=(batch, MLA_HEADS // hps, nq),
        in_specs=[pl.BlockSpec((tq, hps * MLA_QW), lambda b, p, i: (b * nq + i, p))]
        + [kvspec(col) for col in range(2 * hps)]
        + [pl.BlockSpec((seq, LANE), lambda b, p, i: (b, 0))],
        out_specs=pl.BlockSpec((tq, hps * MLA_V), lambda b, p, i: (b * nq + i, p)),
        out_shape=jax.ShapeDtypeStruct((t, MLA_WIDTH), F32),
        scratch_shapes=[
            pltpu.VMEM((hps, tq, tq), F32),
            pltpu.VMEM((hps, tq, 1), F32),
            pltpu.VMEM((hps, tq, 1), F32),
            pltpu.VMEM((hps, tq, MLA_V), F32),
        ],
        compiler_params=_cparams(("parallel", "parallel", "arbitrary")),
        name="mla_attn",
    )(qc, *([kv] * (2 * hps)), kr)


def _cmul(ar, ai, br, bi):
    return ar * br - ai * bi, ar * bi + ai * br


def _s5_prep_kernel(lrc_ref, lic_ref, dtc_ref, ctr_ref, cti_ref, lrr_ref, lir_ref, dtr_ref, btr_ref, bti_ref,
                    tmat_ref, bmat_ref, cmat_ref, pa_ref, pb_ref):
    col_g = lax.broadcasted_iota(jnp.int32, (1, LANE), 1) // SSM_GROUP
    row_g = lax.broadcasted_iota(jnp.int32, (LANE, 1), 0) // SSM_GROUP
    dtc = jnp.exp(dtc_ref[0])
    ar = lrc_ref[0] * dtc
    ai = lic_ref[0] * dtc
    lr, li = jnp.exp(ar) * jnp.cos(ai), jnp.exp(ar) * jnp.sin(ai)
    er, ei = jnp.ones_like(lr), jnp.zeros_like(li)
    vmats = []
    for d in range(S5_L + 1):
        cr, ci = _cmul(ctr_ref[0], cti_ref[0], er, ei)
        vmats.append(jnp.concatenate([cr, -ci], axis=0))
        er, ei = _cmul(er, ei, lr, li)
    zero_blk = jnp.zeros((LANE, LANE), BF16)
    for g in range(S5_GPT):
        for i in range(S5_L):
            cmat_ref[0, g * LANE:(g + 1) * LANE, i * LANE:(i + 1) * LANE] = (
                jnp.where(col_g == g, vmats[i + 1], 0.0).astype(BF16))
    dtr = jnp.exp(dtr_ref[0])
    lam_r = lrr_ref[0]
    lam_i = lir_ref[0]
    arr = lam_r * dtr
    air = lam_i * dtr
    lbr, lbi = jnp.exp(arr) * jnp.cos(air), jnp.exp(arr) * jnp.sin(air)
    nr, ni = lbr - 1.0, lbi
    den = lam_r * lam_r + lam_i * lam_i
    qr = (nr * lam_r + ni * lam_i) / den
    qi = (ni * lam_r - nr * lam_i) / den
    is_re = lax.broadcasted_iota(jnp.int32, (1, 2 * SSM_STATE), 1) < SSM_STATE
    fr, fi = jnp.ones_like(lbr), jnp.zeros_like(lbi)
    bbar_t = None
    for j in range(S5_L - 1, -1, -1):
        wr, wi = _cmul(qr, qi, fr, fi)
        br, bi = _cmul(wr, wi, btr_ref[0], bti_ref[0])
        bj = jnp.where(is_re, br, bi)
        if j == S5_L - 1:
            bbar_t = bj
        for g in range(S5_GPT):
            bmat_ref[0, j * LANE:(j + 1) * LANE, g * LANE:(g + 1) * LANE] = (
                jnp.where(row_g == g, bj, 0.0).astype(BF16))
        fr, fi = _cmul(fr, fi, lbr, lbi)
    cexp = jnp.concatenate(vmats[:S5_L], axis=1)
    dall = jnp.dot(bbar_t, cexp, preferred_element_type=F32, precision=lax.Precision.HIGHEST)
    for d in range(S5_L):
        blk = jnp.where(row_g == col_g, dall[:, d * LANE:(d + 1) * LANE], 0.0).astype(BF16)
        for j in range(S5_L - d):
            i = j + d
            tmat_ref[0, j * LANE:(j + 1) * LANE, i * LANE:(i + 1) * LANE] = blk
    for j in range(1, S5_L):
        for i in range(j):
            tmat_ref[0, j * LANE:(j + 1) * LANE, i * LANE:(i + 1) * LANE] = zero_blk
    zr, zi = fr, fi
    for k in range(pa_ref.shape[1]):
        pa_ref[0, k] = zr
        pb_ref[0, k] = jnp.where(is_re, -zi, zi)
        zr, zi = _cmul(zr, zi, zr, zi)


def _s5_prep(lam_re, lam_im, log_dt, b_re, b_im, c_re, c_im, *, steps):
    g, p = lam_re.shape
    hg = SSM_GROUP
    col = lambda a: jnp.repeat(a.reshape(S5_TILES, S5_GPT, p).transpose(0, 2, 1), hg, axis=2)
    row = lambda a: jnp.tile(jnp.repeat(a.reshape(S5_TILES, S5_GPT, p), hg, axis=1), (1, 1, 2))
    ctc = lambda c: c.reshape(S5_TILES, S5_GPT, hg, p).transpose(0, 3, 1, 2).reshape(S5_TILES, p, LANE)
    btr = lambda b: jnp.tile(b.reshape(S5_TILES, S5_GPT, p, hg).transpose(0, 1, 3, 2).reshape(S5_TILES, LANE, p),
                             (1, 1, 2))
    dt_col = jnp.repeat(log_dt.reshape(S5_TILES, 1, S5_GPT), hg, axis=2)
    dt_row = jnp.repeat(log_dt.reshape(S5_TILES, S5_GPT, 1), hg, axis=1)
    blk = lambda *s: pl.BlockSpec((1,) + s, lambda i: (i,) + (0,) * len(s))
    return pl.pallas_call(
        _s5_prep_kernel,
        grid=(S5_TILES,),
        in_specs=[blk(p, LANE), blk(p, LANE), blk(1, LANE), blk(p, LANE), blk(p, LANE),
                  blk(LANE, 2 * p), blk(LANE, 2 * p), blk(LANE, 1), blk(LANE, 2 * p), blk(LANE, 2 * p)],
        out_specs=[blk(S5_W, S5_W), blk(S5_W, S5_SW), blk(S5_SW, S5_W),
                   blk(steps, LANE, 2 * p), blk(steps, LANE, 2 * p)],
        out_shape=[
            jax.ShapeDtypeStruct((S5_TILES, S5_W, S5_W), BF16),
            jax.ShapeDtypeStruct((S5_TILES, S5_W, S5_SW), BF16),
            jax.ShapeDtypeStruct((S5_TILES, S5_SW, S5_W), BF16),
            jax.ShapeDtypeStruct((S5_TILES, steps, LANE, 2 * p), F32),
            jax.ShapeDtypeStruct((S5_TILES, steps, LANE, 2 * p), F32),
        ],
        compiler_params=_cparams(("parallel",)),
        name="s5_prep",
    )(col(lam_re), col(lam_im), dt_col, ctc(c_re), ctc(c_im),
      row(lam_re), row(lam_im), dt_row, btr(b_re), btr(b_im))


def _s5_scan_kernel(u_ref, tmat_ref, bmat_ref, cmat_ref, pa_ref, pb_ref, y_ref, *, nchunk):
    u = u_ref[0]
    rows = u.shape[0]
    x = jnp.dot(u, bmat_ref[0], preferred_element_type=F32)
    cidx = lax.broadcasted_iota(jnp.int32, (rows, 1), 0) % nchunk
    steps = pa_ref.shape[1]
    s_in = []
    for g in range(S5_GPT):
        xg = x[:, g * 2 * SSM_STATE:(g + 1) * 2 * SSM_STATE]
        r0 = g * SSM_GROUP
        for k in range(steps):
            sh = 1 << k
            xs = jnp.where(cidx >= sh, pltpu.roll(xg, sh, 0), 0.0)
            xg = (xg + pa_ref[0, k, r0:r0 + 1, :] * xs
                  + pb_ref[0, k, r0:r0 + 1, :] * pltpu.roll(xs, SSM_STATE, 1))
        s_in.append(jnp.where(cidx >= 1, pltpu.roll(xg, 1, 0), 0.0).astype(BF16))
    s_in = jnp.concatenate(s_in, axis=1)
    y_ref[0] = jnp.dot(s_in, cmat_ref[0], preferred_element_type=F32)
    nq = 4
    qw = S5_W // nq
    for c in range(nq):
        y_ref[0, :, c * qw:(c + 1) * qw] += jnp.dot(u[:, :(c + 1) * qw], tmat_ref[0, :(c + 1) * qw, c * qw:(c + 1) * qw],
                                                    preferred_element_type=F32)


def _s5_scan(u4, tmat, bmat, cmat, pa, pb, *, nchunk, rows):
    nt, total, w = u4.shape
    return pl.pallas_call(
        functools.partial(_s5_scan_kernel, nchunk=nchunk),
        grid=(nt, total // rows),
        in_specs=[
            pl.BlockSpec((1, rows, w), lambda k, r: (k, r, 0)),
            pl.BlockSpec((1,) + tmat.shape[1:], lambda k, r: (k, 0, 0)),
            pl.BlockSpec((1,) + bmat.shape[1:], lambda k, r: (k, 0, 0)),
            pl.BlockSpec((1,) + cmat.shape[1:], lambda k, r: (k, 0, 0)),
            pl.BlockSpec((1,) + pa.shape[1:], lambda k, r: (k, 0, 0, 0)),
            pl.BlockSpec((1,) + pb.shape[1:], lambda k, r: (k, 0, 0, 0)),
        ],
        out_specs=pl.BlockSpec((1, rows, w), lambda k, r: (k, r, 0)),
        out_shape=jax.ShapeDtypeStruct((nt, total, w), F32),
        compiler_params=_cparams(("parallel", "parallel")),
        name="s5_scan",
    )(u4, tmat, bmat, cmat, pa, pb)


def _outproj_kernel(x_ref, ysb_ref, y4_ref, u_ref, ymla_ref, d_ref, wglu_ref, mnw_ref, wout_ref, o_ref, ys_ref):
    mnw = mnw_ref[...]
    nrow = y4_ref.shape[1]
    for k in range(S5_TILES):
        for j in range(S5_L):
            ys_ref[k, pl.ds(j, nrow, stride=S5_L), :] = y4_ref[k, :, j * LANE:(j + 1) * LANE]
    yscan = jnp.concatenate([ys_ref[k] for k in range(S5_TILES)], axis=1)
    y = yscan + d_ref[...] * u_ref[...]
    g = jax.nn.gelu(y)
    gate = jax.nn.sigmoid(jnp.dot(g.astype(BF16), wglu_ref[...], preferred_element_type=F32))
    o_ssm = SB_WIDTH + SSM_WIDTH
    cat = jnp.concatenate([
        _rms(ysb_ref[...], mnw[:, :SB_WIDTH]).astype(BF16),
        _rms(g * gate, mnw[:, SB_WIDTH:o_ssm]).astype(BF16),
        _rms(ymla_ref[...], mnw[:, o_ssm:]).astype(BF16),
    ], axis=1)
    o_ref[...] = x_ref[...] + jnp.dot(cat, wout_ref[...], preferred_element_type=F32)


def _outproj(x, ysb, y4, u, ymla, d, wglu, mnw, wout, *, tm):
    t, dm = x.shape
    row = lambda i: (i, 0)
    return pl.pallas_call(
        _outproj_kernel,
        grid=(t // tm,),
        in_specs=[
            pl.BlockSpec((tm, dm), row),
            pl.BlockSpec((tm, SB_WIDTH), row),
            pl.BlockSpec((S5_TILES, tm // S5_L, S5_W), lambda i: (0, i, 0)),
            pl.BlockSpec((tm, SSM_WIDTH), row),
            pl.BlockSpec((tm, MLA_WIDTH), row),
            _const_spec((1, SSM_WIDTH)),
            _const_spec(wglu.shape),
            _const_spec((1, mnw.shape[1])),
            _const_spec(wout.shape),
        ],
        out_specs=pl.BlockSpec((tm, dm), row),
        out_shape=jax.ShapeDtypeStruct((t, dm), F32),
        scratch_shapes=[pltpu.VMEM((S5_TILES, tm, LANE), F32)],
        compiler_params=_cparams(("parallel",)),
        name="outproj",
    )(x, ysb, y4, u, ymla, d, wglu, mnw, wout)


FFN_TM = 1024
FFN_TF = 512
CAST_ROWS = 512
PROJ_TM = 512
SB_TQ = 512
SB_TK = 256
MLA_TQ = 512
MLA_HPS = 4
S5_ROWS = 512


def kernel(x, norm_w, ffn_w13, ffn_w2, w_in, mla_q_norm_w, mla_w_uq, mla_kv_norm_w, mla_w_ukv,
           ssm_lam_re, ssm_lam_im, ssm_log_dt, ssm_b_re, ssm_b_im, ssm_c_re, ssm_c_im,
           ssm_d, ssm_w_glu, mix_norm_w, w_out, final_norm_w):
    batch, seq, dm = x.shape
    depth = norm_w.shape[0]
    t = batch * seq
    nchunk = seq // S5_L
    steps = nchunk.bit_length() - 1
    assert 1 << steps == nchunk
    s5_rows = min(S5_ROWS, batch * nchunk)
    assert s5_rows % nchunk == 0
    tabc, tabs = _rope_tables(seq)
    umat = (lax.broadcasted_iota(jnp.int32, (SB_TK, SB_TK), 0)
            > lax.broadcasted_iota(jnp.int32, (SB_TK, SB_TK), 1)).astype(BF16)
    fw = final_norm_w[None, :]
    w13h, w2b = _prep_ffn(ffn_w13, ffn_w2, rows=CAST_ROWS)
    xt = x.reshape(t, dm)
    for l in range(depth):
        xt = _ffn(xt, norm_w[l, 0][None], w13h, w2b, fw, layer=l, pos=0, tm=FFN_TM, tf=FFN_TF, final=False)

        winp, wuqp, wukv = _prep_inproj(w_in[l], mla_w_uq[l], mla_w_ukv[l])
        qkv, u, u4, qc, kv, kr = _inproj(xt, norm_w[l, 1][None], winp, mla_q_norm_w[l][None], wuqp,
                                         mla_kv_norm_w[l][None], wukv, tabc, tabs, tm=PROJ_TM, seq=seq)
        ysb = _sb_attention(qkv, umat, batch=batch, seq=seq, tq=SB_TQ, tk=SB_TK)
        ymla = _mla_attention(qc, kv, kr, batch=batch, seq=seq, tq=MLA_TQ, hps=MLA_HPS)

        tmat, bmat, cmat, pa, pb = _s5_prep(ssm_lam_re[l], ssm_lam_im[l], ssm_log_dt[l],
                                            ssm_b_re[l], ssm_b_im[l], ssm_c_re[l], ssm_c_im[l], steps=steps)
        y4 = _s5_scan(u4, tmat, bmat, cmat, pa, pb, nchunk=nchunk, rows=s5_rows)

        xt = _outproj(xt, ysb, y4, u, ymla, ssm_d[l][None], ssm_w_glu[l].astype(BF16),
                      mix_norm_w[l][None], w_out[l].astype(BF16), tm=PROJ_TM)

        xt = _ffn(xt, norm_w[l, 2][None], w13h, w2b, fw, layer=l, pos=1, tm=FFN_TM, tf=FFN_TF,
                  final=(l == depth - 1))
    return xt.reshape(batch, seq, dm)
```

```python
import functools
import math

import jax
import jax.numpy as jnp
from jax import lax
from jax.experimental import pallas as pl
from jax.experimental.pallas import tpu as pltpu

F32 = jnp.float32
BF16 = jnp.bfloat16

EPS = 1e-6
NEG_INF = -1e30
LOG2E = math.log2(math.e)
LANE = 128
VMEM_LIMIT = 56 * 1024 * 1024

CHUNK = 64
SB_HEADS = 8
SB_HEAD_DIM = 64
SB_WIDTH = SB_HEADS * SB_HEAD_DIM
SSM_WIDTH = 512
SSM_GROUP = 16
SSM_GROUPS = SSM_WIDTH // SSM_GROUP
SSM_STATE = 64
MLA_HEADS = 8
MLA_NOPE = 128
MLA_ROPE = 64
MLA_V = 128
MLA_Q_RANK = 512
MLA_KV_RANK = 256
MLA_WIDTH = MLA_HEADS * MLA_V
ROPE_BASE = 10000.0
ROPE_HALF = MLA_ROPE // 2
MLA_QW = MLA_NOPE + LANE

S5_L = 16
S5_GPT = LANE // SSM_GROUP
S5_TILES = SSM_WIDTH // LANE
S5_W = S5_L * LANE
S5_SW = S5_GPT * 2 * SSM_STATE


def _cparams(sem):
    return pltpu.CompilerParams(dimension_semantics=sem, vmem_limit_bytes=VMEM_LIMIT)


def _rms(x, w):
    ms = jnp.mean(x * x, axis=-1, keepdims=True)
    return x * lax.rsqrt(ms + EPS) * w


def _const_spec(shape):
    nd = len(shape)
    return pl.BlockSpec(shape, lambda *_: (0,) * nd, pipeline_mode=pl.Buffered(1))


def _ffn_kernel(x_ref, nw_ref, w1_ref, w3_ref, w2_ref, fw_ref, o_ref, h_ref, *, last, final):
    j = pl.program_id(1)
    nt = pl.num_programs(1)

    @pl.when(j == 0)
    def _():
        x = x_ref[...]
        h_ref[...] = _rms(x, nw_ref[...]).astype(BF16)
        o_ref[...] = x

    def tile(width):
        h = h_ref[...]
        a = jnp.dot(h, w1_ref[:, :width], preferred_element_type=F32)
        b = jnp.dot(h, w3_ref[:, :width], preferred_element_type=F32)
        g = (a * jax.nn.sigmoid(a)) * b * 0.5
        o_ref[...] += jnp.dot(g.astype(BF16), w2_ref[:width, :], preferred_element_type=F32)

    tf = w1_ref.shape[1]
    if last == tf:
        tile(tf)
    else:
        pl.when(j < nt - 1)(lambda: tile(tf))
        pl.when(j == nt - 1)(lambda: tile(last))

    if final:
        @pl.when(j == nt - 1)
        def _():
            o_ref[...] = _rms(o_ref[...], fw_ref[...])


def _ffn(x, nw, w13h, w2b, fw, *, layer, pos, tm, tf, final):
    t, d = x.shape
    ff = w2b.shape[2]
    nt = pl.cdiv(ff, tf)
    return pl.pallas_call(
        functools.partial(_ffn_kernel, last=ff - (nt - 1) * tf, final=final),
        grid=(t // tm, nt),
        in_specs=[
            pl.BlockSpec((tm, d), lambda i, j: (i, 0)),
            pl.BlockSpec((1, d), lambda i, j: (0, 0)),
            pl.BlockSpec((None, None, None, d, tf), lambda i, j: (layer, pos, 0, 0, j)),
            pl.BlockSpec((None, None, None, d, tf), lambda i, j: (layer, pos, 1, 0, j)),
            pl.BlockSpec((None, None, tf, d), lambda i, j: (layer, pos, j, 0)),
            pl.BlockSpec((1, d), lambda i, j: (0, 0)),
        ],
        out_specs=pl.BlockSpec((tm, d), lambda i, j: (i, 0)),
        out_shape=jax.ShapeDtypeStruct((t, d), F32),
        scratch_shapes=[pltpu.VMEM((tm, d), BF16)],
        compiler_params=_cparams(("parallel", "arbitrary")),
        name="ffn",
    )(x, nw, w13h, w13h, w2b, fw)


def _cast_kernel(w_ref, o_ref):
    o_ref[...] = w_ref[...].astype(o_ref.dtype)


def _prep_ffn(ffn_w13, ffn_w2, *, rows):
    depth, npos, d, two_ff = ffn_w13.shape
    ff = two_ff // 2
    w13h = pl.pallas_call(
        _cast_kernel,
        grid=(depth, npos, 2, d // rows),
        in_specs=[pl.BlockSpec((None, None, rows, ff), lambda l, k, h, r: (l, k, r, h))],
        out_specs=pl.BlockSpec((None, None, None, rows, ff), lambda l, k, h, r: (l, k, h, r, 0)),
        out_shape=jax.ShapeDtypeStruct((depth, npos, 2, d, ff), BF16),
        compiler_params=_cparams(("parallel",) * 4),
        name="w13_cast",
    )(ffn_w13)
    return w13h, ffn_w2.astype(BF16)


def _rope(r, tc, ts):
    return r * tc + pltpu.roll(r, 2 * ROPE_HALF, 1) * ts


def _inproj_kernel(x_ref, nw_ref, win_ref, qnw_ref, wuq_ref, kvnw_ref, wukv_ref, tc_ref, ts_ref,
                   qkv_ref, u_ref, u4_ref, qc_ref, kv_ref, kr_ref, us_ref):
    h = _rms(x_ref[...], nw_ref[...]).astype(BF16)
    y = jnp.dot(h, win_ref[...], preferred_element_type=F32)
    o_u = 3 * SB_WIDTH
    o_cq = o_u + SSM_WIDTH
    o_ckv = o_cq + MLA_Q_RANK
    o_kr = o_ckv + MLA_KV_RANK
    qkv_ref[:, :SB_WIDTH] = (y[:, :SB_WIDTH] * (SB_HEAD_DIM ** -0.5)).astype(BF16)
    qkv_ref[:, SB_WIDTH:] = y[:, SB_WIDTH:o_u].astype(BF16)
    u = y[:, o_u:o_cq]
    u_ref[...] = u
    nrow = u.shape[0] // S5_L
    for k in range(S5_TILES):
        us_ref[k] = u[:, k * LANE:(k + 1) * LANE]
        for j in range(S5_L):
            u4_ref[k, :, j * LANE:(j + 1) * LANE] = us_ref[k, pl.ds(j, nrow, stride=S5_L), :].astype(BF16)
    tc = tc_ref[...]
    ts = ts_ref[...]
    cq = _rms(y[:, o_cq:o_ckv], qnw_ref[...]).astype(BF16)
    q = jnp.dot(cq, wuq_ref[...], preferred_element_type=F32) * ((MLA_NOPE + MLA_ROPE) ** -0.5 * LOG2E)
    for hh in range(MLA_HEADS):
        o = hh * MLA_QW
        qc_ref[:, o:o + MLA_NOPE] = q[:, o:o + MLA_NOPE].astype(BF16)
        qc_ref[:, o + MLA_NOPE:o + MLA_QW] = _rope(q[:, o + MLA_NOPE:o + MLA_QW], tc, ts).astype(BF16)
    ckv = _rms(y[:, o_ckv:o_kr], kvnw_ref[...]).astype(BF16)
    kv_ref[...] = jnp.dot(ckv, wukv_ref[...], preferred_element_type=F32).astype(BF16)
    kr_ref[...] = _rope(y[:, o_kr:o_kr + LANE], tc, ts).astype(BF16)


def _inproj(x, nw, winp, qnw, wuqp, kvnw, wukv, tabc, tabs, *, tm, seq):
    t, d = x.shape
    nseq = seq // tm
    row = lambda i: (i, 0)
    return pl.pallas_call(
        _inproj_kernel,
        grid=(t // tm,),
        in_specs=[
            pl.BlockSpec((tm, d), row),
            _const_spec((1, d)),
            _const_spec(winp.shape),
            _const_spec((1, MLA_Q_RANK)),
            _const_spec(wuqp.shape),
            _const_spec((1, MLA_KV_RANK)),
            _const_spec(wukv.shape),
            pl.BlockSpec((tm, LANE), lambda i: (i % nseq, 0)),
            pl.BlockSpec((tm, LANE), lambda i: (i % nseq, 0)),
        ],
        out_specs=[
            pl.BlockSpec((tm, 3 * SB_WIDTH), row),
            pl.BlockSpec((tm, SSM_WIDTH), row),
            pl.BlockSpec((S5_TILES, tm // S5_L, S5_W), lambda i: (0, i, 0)),
            pl.BlockSpec((tm, MLA_HEADS * MLA_QW), row),
            pl.BlockSpec((tm, MLA_HEADS * (MLA_NOPE + MLA_V)), row),
            pl.BlockSpec((tm, LANE), row),
        ],
        out_shape=[
            jax.ShapeDtypeStruct((t, 3 * SB_WIDTH), BF16),
            jax.ShapeDtypeStruct((t, SSM_WIDTH), F32),
            jax.ShapeDtypeStruct((S5_TILES, t // S5_L, S5_W), BF16),
            jax.ShapeDtypeStruct((t, MLA_HEADS * MLA_QW), BF16),
            jax.ShapeDtypeStruct((t, MLA_HEADS * (MLA_NOPE + MLA_V)), BF16),
            jax.ShapeDtypeStruct((t, LANE), BF16),
        ],
        scratch_shapes=[pltpu.VMEM((S5_TILES, tm, LANE), F32)],
        compiler_params=_cparams(("parallel",)),
        name="inproj",
    )(x, nw, winp, qnw, wuqp, kvnw, wukv, tabc, tabs)


def _rope_pad(w):
    z = jnp.zeros(w.shape[:-1] + (ROPE_HALF,), w.dtype)
    return jnp.concatenate([w[..., :ROPE_HALF], z, w[..., ROPE_HALF:], z], axis=-1)


def _prep_inproj(w_in, w_uq, w_ukv):
    o_kr = 3 * SB_WIDTH + SSM_WIDTH + MLA_Q_RANK + MLA_KV_RANK
    w_in = w_in.astype(BF16)
    winp = jnp.concatenate([w_in[:, :o_kr], _rope_pad(w_in[:, o_kr:])], axis=1)
    wq = w_uq.astype(BF16).reshape(MLA_Q_RANK, MLA_HEADS, MLA_NOPE + MLA_ROPE)
    wuqp = jnp.concatenate([wq[..., :MLA_NOPE], _rope_pad(wq[..., MLA_NOPE:])], axis=-1)
    wuqp = wuqp.reshape(MLA_Q_RANK, MLA_HEADS * MLA_QW)
    return winp, wuqp, w_ukv.astype(BF16)


def _rope_tables(seq):
    inv = ROPE_BASE ** (-jnp.arange(0, MLA_ROPE, 2, dtype=F32) / MLA_ROPE)
    ang = jnp.arange(seq, dtype=F32)[:, None] * inv[None, :]
    c, s = jnp.cos(ang), jnp.sin(ang)
    z = jnp.zeros_like(c)
    return jnp.concatenate([c, z, c, z], axis=1), jnp.concatenate([-s, z, s, z], axis=1)


def _sb_kernel(q_ref, k_ref, v_ref, u_ref, o_ref, z_s, lb_s, sp_s, cq_s, cb_s, w_s, acc_s, *, tq, tk):
    i = pl.program_id(2)
    nblk = (i + 1) * (tq // tk)
    assert tq // tk == 2
    first_head = lax.broadcasted_iota(jnp.int32, (1, LANE), 1) < SB_HEAD_DIM
    q = q_ref[...]
    zero = jnp.zeros_like(q)
    q2 = jnp.concatenate([jnp.where(first_head, q, zero), jnp.where(first_head, zero, q)], axis=0)
    umat = u_ref[...]
    rowpos = i * tq + lax.broadcasted_iota(jnp.int32, (2 * tq, 1), 0) % tq
    colofs = lax.broadcasted_iota(jnp.int32, (1, tk), 1)

    def start(t):
        return pl.multiple_of((nblk - 1 - t) * tk, tk)

    def s1(t):
        z_s[...] = lax.dot_general(q2, k_ref[pl.ds(start(t), tk), :], (((1,), (1,)), ((), ())),
                                   preferred_element_type=F32)

    def s2(t, masked):
        z = z_s[...]
        sp = jnp.maximum(z, 0.0) + jnp.log(1.0 + jnp.exp(-jnp.abs(z)))
        lb = z - sp
        if masked:
            strict = (start(t) + colofs) < rowpos
            sp = jnp.where(strict, sp, 0.0)
            lb = jnp.where(strict, lb, NEG_INF)
        lb_s[...] = lb
        sp_s[...] = sp.astype(BF16)
        carry = cb_s[...]
        cq_s[...] = carry
        cb_s[...] = carry + jnp.sum(sp, axis=-1, keepdims=True)

    def s3():
        tail = jnp.dot(sp_s[...], umat, preferred_element_type=F32)
        w_s[...] = jnp.exp(lb_s[...] - tail - cq_s[...]).astype(BF16)

    def s4(t):
        acc_s[...] += jnp.dot(w_s[...], v_ref[pl.ds(start(t), tk), :], preferred_element_type=F32)

    cb_s[...] = jnp.zeros_like(cb_s)
    acc_s[...] = jnp.zeros_like(acc_s)
    s1(0)
    s2(0, True)
    s1(1)
    s3()
    s2(1, True)

    @pl.when(i > 0)
    def _():
        s1(2)

        def body(t, c):
            s4(t - 3)
            s3()
            s2(t - 1, False)
            s1(t)
            return c

        lax.fori_loop(3, nblk, body, 0)
        s4(nblk - 3)
        s3()
        s2(nblk - 1, False)

    s4(nblk - 2)
    s3()
    s4(nblk - 1)
    acc = acc_s[...]
    o_ref[...] = jnp.where(first_head, acc[:tq], acc[tq:])


def _sb_attention(qkv, umat, *, batch, seq, tq, tk):
    t = qkv.shape[0]
    nq = seq // tq
    npair = SB_WIDTH // LANE
    return pl.pallas_call(
        functools.partial(_sb_kernel, tq=tq, tk=tk),
        grid=(batch, npair, nq),
        in_specs=[
            pl.BlockSpec((tq, LANE), lambda b, p, i: (b * nq + i, p)),
            pl.BlockSpec((seq, LANE), lambda b, p, i: (b, npair + p)),
            pl.BlockSpec((seq, LANE), lambda b, p, i: (b, 2 * npair + p)),
            pl.BlockSpec((tk, tk), lambda b, p, i: (0, 0)),
        ],
        out_specs=pl.BlockSpec((tq, LANE), lambda b, p, i: (b * nq + i, p)),
        out_shape=jax.ShapeDtypeStruct((t, SB_WIDTH), F32),
        scratch_shapes=[
            pltpu.VMEM((2 * tq, tk), F32),
            pltpu.VMEM((2 * tq, tk), F32),
            pltpu.VMEM((2 * tq, tk), BF16),
            pltpu.VMEM((2 * tq, 1), F32),
            pltpu.VMEM((2 * tq, 1), F32),
            pltpu.VMEM((2 * tq, tk), BF16),
            pltpu.VMEM((2 * tq, LANE), F32),
        ],
        compiler_params=_cparams(("parallel", "parallel", "arbitrary")),
        name="sb_attn",
    )(qkv, qkv, qkv, umat)


def _mla_kernel(q_ref, *refs, tq, hps):
    kv_refs, (kr_ref, o_ref, s_s, m_s, acc_s) = refs[:2 * hps], refs[2 * hps:]
    i = pl.program_id(2)
    tk = tq
    q = q_ref[...]
    heads = tuple((q[:, hh * MLA_QW:(hh + 1) * MLA_QW], kv_refs[2 * hh], kv_refs[2 * hh + 1]) for hh in range(hps))
    qchunk = (i * tq + lax.broadcasted_iota(jnp.int32, (tq, 1), 0)) // CHUNK
    colofs = lax.broadcasted_iota(jnp.int32, (1, tk), 1)

    def start(t):
        return pl.multiple_of((i - t) * tk, tk)

    def scores(t, masked):
        ks = start(t)
        krb = kr_ref[pl.ds(ks, tk), :]
        for hh, (qh, kn_ref, _) in enumerate(heads):
            kcat = jnp.concatenate([kn_ref[pl.ds(ks, tk), :], krb], axis=1)
            s = lax.dot_general(qh, kcat, (((1,), (1,)), ((), ())), preferred_element_type=F32)
            if masked:
                s = jnp.where((ks + colofs) // CHUNK <= qchunk, s, NEG_INF)
            s_s[hh] = s

    ones = jnp.ones((tk, LANE), BF16)

    def update(t):
        for hh, (_, _, v_ref) in enumerate(heads):
            s = s_s[hh]
            m = m_s[hh]
            m_new = jnp.maximum(m, jnp.max(s, axis=-1, keepdims=True))
            alpha = jnp.exp2(m - m_new)
            p = jnp.exp2(s - m_new)
            m_s[hh] = m_new
            v_aug = jnp.concatenate([v_ref[pl.ds(start(t), tk), :], ones], axis=1)
            acc_s[hh] = alpha * acc_s[hh] + jnp.dot(p.astype(BF16), v_aug, preferred_element_type=F32)

    m_s[...] = jnp.full(m_s.shape, NEG_INF, F32)
    acc_s[...] = jnp.zeros_like(acc_s)
    scores(0, True)

    def body(t, c):
        update(t - 1)
        scores(t, False)
        return c

    lax.fori_loop(1, i + 1, body, 0)
    update(i)
    o_ref[...] = jnp.concatenate([acc_s[hh, :, :MLA_V] / acc_s[hh, :, MLA_V:] for hh in range(hps)], axis=1)


def _mla_attention(qc, kv, kr, *, batch, seq, tq, hps):
    t = qc.shape[0]
    nq = seq // tq
    kvspec = lambda col: pl.BlockSpec((seq, LANE), lambda b, p, i: (b, 2 * hps * p + col))
    return pl.pallas_call(
        functools.partial(_mla_kernel, tq=tq, hps=hps),
        grid=(batch, MLA_HEADS // hps, nq),
        in_specs=[pl.BlockSpec((tq, hps * MLA_QW), lambda b, p, i: (b * nq + i, p))]
        + [kvspec(col) for col in range(2 * hps)]
        + [pl.BlockSpec((seq, LANE), lambda b, p, i: (b, 0))],
        out_specs=pl.BlockSpec((tq, hps * MLA_V), lambda b, p, i: (b * nq + i, p)),
        out_shape=jax.ShapeDtypeStruct((t, MLA_WIDTH), F32),
        scratch_shapes=[
            pltpu.VMEM((hps, tq, tq), F32),
            pltpu.VMEM((hps, tq, 1), F32),
            pltpu.VMEM((hps, tq, MLA_V + LANE), F32),
        ],
        compiler_params=_cparams(("parallel", "parallel", "arbitrary")),
        name="mla_attn",
    )(qc, *([kv] * (2 * hps)), kr)


def _cmul(ar, ai, br, bi):
    return ar * br - ai * bi, ar * bi + ai * br


def _s5_prep_kernel(lrc_ref, lic_ref, dtc_ref, ctr_ref, cti_ref, lrr_ref, lir_ref, dtr_ref, btr_ref, bti_ref,
                    tmat_ref, bmat_ref, cmat_ref, pa_ref, pb_ref):
    col_g = lax.broadcasted_iota(jnp.int32, (1, LANE), 1) // SSM_GROUP
    row_g = lax.broadcasted_iota(jnp.int32, (LANE, 1), 0) // SSM_GROUP
    dtc = jnp.exp(dtc_ref[0])
    ar = lrc_ref[0] * dtc
    ai = lic_ref[0] * dtc
    lr, li = jnp.exp(ar) * jnp.cos(ai), jnp.exp(ar) * jnp.sin(ai)
    er, ei = jnp.ones_like(lr), jnp.zeros_like(li)
    vmats = []
    for d in range(S5_L + 1):
        cr, ci = _cmul(ctr_ref[0], cti_ref[0], er, ei)
        vmats.append(jnp.concatenate([cr, -ci], axis=0))
        er, ei = _cmul(er, ei, lr, li)
    zero_blk = jnp.zeros((LANE, LANE), BF16)
    for g in range(S5_GPT):
        for i in range(S5_L):
            cmat_ref[0, g * LANE:(g + 1) * LANE, i * LANE:(i + 1) * LANE] = (
                jnp.where(col_g == g, vmats[i + 1], 0.0).astype(BF16))
    dtr = jnp.exp(dtr_ref[0])
    lam_r = lrr_ref[0]
    lam_i = lir_ref[0]
    arr = lam_r * dtr
    air = lam_i * dtr
    lbr, lbi = jnp.exp(arr) * jnp.cos(air), jnp.exp(arr) * jnp.sin(air)
    nr, ni = lbr - 1.0, lbi
    den = lam_r * lam_r + lam_i * lam_i
    qr = (nr * lam_r + ni * lam_i) / den
    qi = (ni * lam_r - nr * lam_i) / den
    is_re = lax.broadcasted_iota(jnp.int32, (1, 2 * SSM_STATE), 1) < SSM_STATE
    fr, fi = jnp.ones_like(lbr), jnp.zeros_like(lbi)
    bbar_t = None
    for j in range(S5_L - 1, -1, -1):
        wr, wi = _cmul(qr, qi, fr, fi)
        br, bi = _cmul(wr, wi, btr_ref[0], bti_ref[0])
        bj = jnp.where(is_re, br, bi)
        if j == S5_L - 1:
            bbar_t = bj
        for g in range(S5_GPT):
            bmat_ref[0, j * LANE:(j + 1) * LANE, g * LANE:(g + 1) * LANE] = (
                jnp.where(row_g == g, bj, 0.0).astype(BF16))
        fr, fi = _cmul(fr, fi, lbr, lbi)
    cexp = jnp.concatenate(vmats[:S5_L], axis=1)
    dall = jnp.dot(bbar_t, cexp, preferred_element_type=F32, precision=lax.Precision.HIGHEST)
    for d in range(S5_L):
        blk = jnp.where(row_g == col_g, dall[:, d * LANE:(d + 1) * LANE], 0.0).astype(BF16)
        for j in range(S5_L - d):
            i = j + d
            tmat_ref[0, j * LANE:(j + 1) * LANE, i * LANE:(i + 1) * LANE] = blk
    for j in range(1, S5_L):
        for i in range(j):
            tmat_ref[0, j * LANE:(j + 1) * LANE, i * LANE:(i + 1) * LANE] = zero_blk
    zr, zi = fr, fi
    for k in range(pa_ref.shape[1]):
        pa_ref[0, k] = zr
        pb_ref[0, k] = jnp.where(is_re, -zi, zi)
        zr, zi = _cmul(zr, zi, zr, zi)


def _s5_prep(lam_re, lam_im, log_dt, b_re, b_im, c_re, c_im, *, steps):
    g, p = lam_re.shape
    hg = SSM_GROUP
    col = lambda a: jnp.repeat(a.reshape(S5_TILES, S5_GPT, p).transpose(0, 2, 1), hg, axis=2)
    row = lambda a: jnp.tile(jnp.repeat(a.reshape(S5_TILES, S5_GPT, p), hg, axis=1), (1, 1, 2))
    ctc = lambda c: c.reshape(S5_TILES, S5_GPT, hg, p).transpose(0, 3, 1, 2).reshape(S5_TILES, p, LANE)
    btr = lambda b: jnp.tile(b.reshape(S5_TILES, S5_GPT, p, hg).transpose(0, 1, 3, 2).reshape(S5_TILES, LANE, p),
                             (1, 1, 2))
    dt_col = jnp.repeat(log_dt.reshape(S5_TILES, 1, S5_GPT), hg, axis=2)
    dt_row = jnp.repeat(log_dt.reshape(S5_TILES, S5_GPT, 1), hg, axis=1)
    blk = lambda *s: pl.BlockSpec((1,) + s, lambda i: (i,) + (0,) * len(s))
    return pl.pallas_call(
        _s5_prep_kernel,
        grid=(S5_TILES,),
        in_specs=[blk(p, LANE), blk(p, LANE), blk(1, LANE), blk(p, LANE), blk(p, LANE),
                  blk(LANE, 2 * p), blk(LANE, 2 * p), blk(LANE, 1), blk(LANE, 2 * p), blk(LANE, 2 * p)],
        out_specs=[blk(S5_W, S5_W), blk(S5_W, S5_SW), blk(S5_SW, S5_W),
                   blk(steps, LANE, 2 * p), blk(steps, LANE, 2 * p)],
        out_shape=[
            jax.ShapeDtypeStruct((S5_TILES, S5_W, S5_W), BF16),
            jax.ShapeDtypeStruct((S5_TILES, S5_W, S5_SW), BF16),
            jax.ShapeDtypeStruct((S5_TILES, S5_SW, S5_W), BF16),
            jax.ShapeDtypeStruct((S5_TILES, steps, LANE, 2 * p), F32),
            jax.ShapeDtypeStruct((S5_TILES, steps, LANE, 2 * p), F32),
        ],
        compiler_params=_cparams(("parallel",)),
        name="s5_prep",
    )(col(lam_re), col(lam_im), dt_col, ctc(c_re), ctc(c_im),
      row(lam_re), row(lam_im), dt_row, btr(b_re), btr(b_im))


def _s5_scan_kernel(u_ref, tmat_ref, bmat_ref, cmat_ref, pa_ref, pb_ref, y_ref, *, nchunk):
    u = u_ref[0]
    rows = u.shape[0]
    x = jnp.dot(u, bmat_ref[0], preferred_element_type=F32)
    cidx = lax.broadcasted_iota(jnp.int32, (rows, 1), 0) % nchunk
    steps = pa_ref.shape[1]
    s_in = []
    for g in range(S5_GPT):
        xg = x[:, g * 2 * SSM_STATE:(g + 1) * 2 * SSM_STATE]
        r0 = g * SSM_GROUP
        for k in range(steps):
            sh = 1 << k
            xs = jnp.where(cidx >= sh, pltpu.roll(xg, sh, 0), 0.0)
            xg = (xg + pa_ref[0, k, r0:r0 + 1, :] * xs
                  + pb_ref[0, k, r0:r0 + 1, :] * pltpu.roll(xs, SSM_STATE, 1))
        s_in.append(jnp.where(cidx >= 1, pltpu.roll(xg, 1, 0), 0.0).astype(BF16))
    s_in = jnp.concatenate(s_in, axis=1)
    y_ref[0] = jnp.dot(s_in, cmat_ref[0], preferred_element_type=F32)
    nq = 4
    qw = S5_W // nq
    for c in range(nq):
        y_ref[0, :, c * qw:(c + 1) * qw] += jnp.dot(u[:, :(c + 1) * qw], tmat_ref[0, :(c + 1) * qw, c * qw:(c + 1) * qw],
                                                    preferred_element_type=F32)


def _s5_scan(u4, tmat, bmat, cmat, pa, pb, *, nchunk, rows):
    nt, total, w = u4.shape
    return pl.pallas_call(
        functools.partial(_s5_scan_kernel, nchunk=nchunk),
        grid=(nt, total // rows),
        in_specs=[
            pl.BlockSpec((1, rows, w), lambda k, r: (k, r, 0)),
            pl.BlockSpec((1,) + tmat.shape[1:], lambda k, r: (k, 0, 0)),
            pl.BlockSpec((1,) + bmat.shape[1:], lambda k, r: (k, 0, 0)),
            pl.BlockSpec((1,) + cmat.shape[1:], lambda k, r: (k, 0, 0)),
            pl.BlockSpec((1,) + pa.shape[1:], lambda k, r: (k, 0, 0, 0)),
            pl.BlockSpec((1,) + pb.shape[1:], lambda k, r: (k, 0, 0, 0)),
        ],
        out_specs=pl.BlockSpec((1, rows, w), lambda k, r: (k, r, 0)),
        out_shape=jax.ShapeDtypeStruct((nt, total, w), F32),
        compiler_params=_cparams(("parallel", "parallel")),
        name="s5_scan",
    )(u4, tmat, bmat, cmat, pa, pb)


def _outproj_kernel(x_ref, ysb_ref, y4_ref, u_ref, ymla_ref, d_ref, wglu_ref, mnw_ref, wout_ref, o_ref, ys_ref):
    mnw = mnw_ref[...]
    nrow = y4_ref.shape[1]
    for k in range(S5_TILES):
        for j in range(S5_L):
            ys_ref[k, pl.ds(j, nrow, stride=S5_L), :] = y4_ref[k, :, j * LANE:(j + 1) * LANE]
    yscan = jnp.concatenate([ys_ref[k] for k in range(S5_TILES)], axis=1)
    y = yscan + d_ref[...] * u_ref[...]
    g = jax.nn.gelu(y)
    gate = jax.nn.sigmoid(jnp.dot(g.astype(BF16), wglu_ref[...], preferred_element_type=F32))
    o_ssm = SB_WIDTH + SSM_WIDTH
    cat = jnp.concatenate([
        _rms(ysb_ref[...], mnw[:, :SB_WIDTH]).astype(BF16),
        _rms(g * gate, mnw[:, SB_WIDTH:o_ssm]).astype(BF16),
        _rms(ymla_ref[...], mnw[:, o_ssm:]).astype(BF16),
    ], axis=1)
    o_ref[...] = x_ref[...] + jnp.dot(cat, wout_ref[...], preferred_element_type=F32)


def _outproj(x, ysb, y4, u, ymla, d, wglu, mnw, wout, *, tm):
    t, dm = x.shape
    row = lambda i: (i, 0)
    return pl.pallas_call(
        _outproj_kernel,
        grid=(t // tm,),
        in_specs=[
            pl.BlockSpec((tm, dm), row),
            pl.BlockSpec((tm, SB_WIDTH), row),
            pl.BlockSpec((S5_TILES, tm // S5_L, S5_W), lambda i: (0, i, 0)),
            pl.BlockSpec((tm, SSM_WIDTH), row),
            pl.BlockSpec((tm, MLA_WIDTH), row),
            _const_spec((1, SSM_WIDTH)),
            _const_spec(wglu.shape),
            _const_spec((1, mnw.shape[1])),
            _const_spec(wout.shape),
        ],
        out_specs=pl.BlockSpec((tm, dm), row),
        out_shape=jax.ShapeDtypeStruct((t, dm), F32),
        scratch_shapes=[pltpu.VMEM((S5_TILES, tm, LANE), F32)],
        compiler_params=_cparams(("parallel",)),
        name="outproj",
    )(x, ysb, y4, u, ymla, d, wglu, mnw, wout)


FFN_TM = 1024
FFN_TF = 512
CAST_ROWS = 512
PROJ_TM = 512
SB_TQ = 512
SB_TK = 256
MLA_TQ = 512
MLA_HPS = 4
S5_ROWS = 512


def kernel(x, norm_w, ffn_w13, ffn_w2, w_in, mla_q_norm_w, mla_w_uq, mla_kv_norm_w, mla_w_ukv,
           ssm_lam_re, ssm_lam_im, ssm_log_dt, ssm_b_re, ssm_b_im, ssm_c_re, ssm_c_im,
           ssm_d, ssm_w_glu, mix_norm_w, w_out, final_norm_w):
    batch, seq, dm = x.shape
    depth = norm_w.shape[0]
    t = batch * seq
    nchunk = seq // S5_L
    steps = nchunk.bit_length() - 1
    assert 1 << steps == nchunk
    s5_rows = min(S5_ROWS, batch * nchunk)
    assert s5_rows % nchunk == 0
    tabc, tabs = _rope_tables(seq)
    umat = (lax.broadcasted_iota(jnp.int32, (SB_TK, SB_TK), 0)
            > lax.broadcasted_iota(jnp.int32, (SB_TK, SB_TK), 1)).astype(BF16)
    fw = final_norm_w[None, :]
    w13h, w2b = _prep_ffn(ffn_w13, ffn_w2, rows=CAST_ROWS)
    xt = x.reshape(t, dm)
    for l in range(depth):
        xt = _ffn(xt, norm_w[l, 0][None], w13h, w2b, fw, layer=l, pos=0, tm=FFN_TM, tf=FFN_TF, final=False)

        winp, wuqp, wukv = _prep_inproj(w_in[l], mla_w_uq[l], mla_w_ukv[l])
        qkv, u, u4, qc, kv, kr = _inproj(xt, norm_w[l, 1][None], winp, mla_q_norm_w[l][None], wuqp,
                                         mla_kv_norm_w[l][None], wukv, tabc, tabs, tm=PROJ_TM, seq=seq)
        ysb = _sb_attention(qkv, umat, batch=batch, seq=seq, tq=SB_TQ, tk=SB_TK)
        ymla = _mla_attention(qc, kv, kr, batch=batch, seq=seq, tq=MLA_TQ, hps=MLA_HPS)

        tmat, bmat, cmat, pa, pb = _s5_prep(ssm_lam_re[l], ssm_lam_im[l], ssm_log_dt[l],
                                            ssm_b_re[l], ssm_b_im[l], ssm_c_re[l], ssm_c_im[l], steps=steps)
        y4 = _s5_scan(u4, tmat, bmat, cmat, pa, pb, nchunk=nchunk, rows=s5_rows)

        xt = _outproj(xt, ysb, y4, u, ymla, ssm_d[l][None], ssm_w_glu[l].astype(BF16),
                      mix_norm_w[l][None], w_out[l].astype(BF16), tm=PROJ_TM)

        xt = _ffn(xt, norm_w[l, 2][None], w13h, w2b, fw, layer=l, pos=1, tm=FFN_TM, tf=FFN_TF,
                  final=(l == depth - 1))
    return xt.reshape(batch, seq, dm)
```
